```python
import math
import jax
import jax.numpy as jnp
from jax import lax
import numpy as np

D_MODEL = 1024
BATCH = 4
SEQ = 4096
DEPTH = 4

CTX_LEN = 256
GRID_W = 64

MLA_HEADS = 4
MLA_NOPE = 128
MLA_ROPE = 64
MLA_V = 128
MLA_Q_LORA = 256
MLA_KV_LORA = 128
RET_HEADS = 4
RET_DK = 128
RET_DV = 128
GDN_HEADS = 4
GDN_DK = 128
GDN_DV = 128
GDN_CONV = 5

MLA_W = MLA_HEADS * MLA_V
RET_W = RET_HEADS * RET_DV
GDN_W = GDN_HEADS * GDN_DV
MIX_W = MLA_W + RET_W + GDN_W
GDN_CONV_CH = 2 * GDN_HEADS * GDN_DK + GDN_W

IN_SIZES = (MLA_Q_LORA, MLA_KV_LORA, MLA_ROPE,
            RET_HEADS * RET_DK, RET_HEADS * RET_DK, RET_W, RET_W,
            GDN_HEADS * GDN_DK, GDN_HEADS * GDN_DK, GDN_W, GDN_W,
            2 * GDN_HEADS, 2 * GDN_HEADS)
IN_W = sum(IN_SIZES)

N_EXPERTS = 32
TOP_K = 4
D_EXPERT = 1024
SWIGLU_LIMIT = 7.0
SWIGLU_ALPHA = 1.702
MOE_BLOCK = 256

CHUNK = 64
Q_BLOCK = 128
ROPE_BASE = 10000.0
NORM_EPS = 1e-6

kernel_name = "hybrid_mla_retention_gdn_moe_dit"


def rms_norm(x, g, eps=NORM_EPS):
    xf = x.astype(jnp.float32)
    y = xf * lax.rsqrt(jnp.mean(xf * xf, -1, keepdims=True) + eps)
    return (y * g.astype(jnp.float32)).astype(x.dtype)


def l2norm(a, eps=1e-6):
    return a * lax.rsqrt(jnp.sum(a * a, -1, keepdims=True) + eps)


def axial_rope(x, rows, cols):
    d = x.shape[-1]
    half = d // 2
    quarter = half // 2
    inv = ROPE_BASE ** (-jnp.arange(quarter, dtype=jnp.float32) / quarter)
    xf = x.astype(jnp.float32)

    def rot(xh, p):
        ang = p.astype(jnp.float32)[:, None] * inv[None, :]
        cos = jnp.cos(ang)[None, :, None, :]
        sin = jnp.sin(ang)[None, :, None, :]
        x1, x2 = xh[..., :quarter], xh[..., quarter:]
        return jnp.concatenate([x1 * cos - x2 * sin, x2 * cos + x1 * sin], -1)

    out = jnp.concatenate([rot(xf[..., :half], rows), rot(xf[..., half:], cols)], -1)
    return out.astype(x.dtype)


def split_columns(p):
    idx = []
    acc = 0
    for sz in IN_SIZES[:-1]:
        acc += sz
        idx.append(acc)
    return jnp.split(p, idx, axis=-1)


def depthwise_conv(x, w):
    k = w.shape[0]
    return lax.conv_general_dilated(x, w[:, None, :], window_strides=(1,), padding=((k // 2, k // 2),),
                                    dimension_numbers=('NWC', 'WIO', 'NWC'), feature_group_count=x.shape[-1])


def softmax_attend(q, k, v):
    s = jnp.einsum('bqhd,bkhd->bhqk', q, k).astype(jnp.float32) * (q.shape[-1] ** -0.5)
    p = jax.nn.softmax(s, axis=-1).astype(v.dtype)
    return jnp.einsum('bhqk,bkhd->bqhd', p, v)


def mla_qkv(dq, dkv, kr, g_cq, g_ckv, w_uq, w_ukv, pos):
    b, t = dq.shape[:2]
    q = (rms_norm(dq, g_cq) @ w_uq).reshape(b, t, MLA_HEADS, MLA_NOPE + MLA_ROPE)
    kv = (rms_norm(dkv, g_ckv) @ w_ukv).reshape(b, t, MLA_HEADS, MLA_NOPE + MLA_V)
    q_nope, q_rope = q[..., :MLA_NOPE], q[..., MLA_NOPE:]
    k_nope, v = kv[..., :MLA_NOPE], kv[..., MLA_NOPE:]
    k_rope = kr[:, :, None, :]
    if pos is not None:
        q_rope = axial_rope(q_rope, pos[0], pos[1])
        k_rope = axial_rope(k_rope, pos[0], pos[1])
    k_rope = jnp.broadcast_to(k_rope, (b, t, MLA_HEADS, MLA_ROPE))
    return (jnp.concatenate([q_nope, q_rope], -1), jnp.concatenate([k_nope, k_rope], -1), v)


def mla_mixer(pc, pl, pos, g_cq, g_ckv, w_uq, w_ukv, g_out):
    q_c, k_c, v_c = mla_qkv(pc[0], pc[1], pc[2], g_cq, g_ckv, w_uq, w_ukv, None)
    q_l, k_l, v_l = mla_qkv(pl[0], pl[1], pl[2], g_cq, g_ckv, w_uq, w_ukv, pos)
    b, s = q_l.shape[:2]
    o_c = softmax_attend(q_c, k_c, v_c).reshape(b, -1, MLA_W)
    k_all = jnp.concatenate([k_c, k_l], 1)
    v_all = jnp.concatenate([v_c, v_l], 1)
    nb = s // Q_BLOCK
    q_blocks = jnp.moveaxis(q_l.reshape(b, nb, Q_BLOCK, MLA_HEADS, -1), 1, 0)
    o_l = lax.map(lambda qb: softmax_attend(qb, k_all, v_all), q_blocks)
    o_l = jnp.moveaxis(o_l, 0, 1).reshape(b, s, MLA_W)
    return rms_norm(o_c, g_out), rms_norm(o_l, g_out)


def flip_t(arrs):
    return tuple(jnp.flip(a, 1) for a in arrs)


def prefix_bidirectional(scan_f, scan_b, ctx_f, ctx_b, lat_f, lat_b, s0):
    o_cf, s_cf = scan_f(*ctx_f, s0)
    o_cb, s_cb = scan_b(*flip_t(ctx_b), s0)
    o_lf, _ = scan_f(*lat_f, s_cf)
    o_lb, _ = scan_b(*flip_t(lat_b), s_cb)
    return o_cf + jnp.flip(o_cb, 1), o_lf + jnp.flip(o_lb, 1)


def retention_chunked(q, k, v, log_gamma, s0):
    b, t, h, dk = q.shape
    dv = v.shape[-1]
    n = t // CHUNK
    qc = q.reshape(b, n, CHUNK, h, dk)
    kc = k.reshape(b, n, CHUNK, h, dk)
    vc = v.reshape(b, n, CHUNK, h, dv)
    pos = jnp.arange(CHUNK, dtype=jnp.float32)
    diff = pos[:, None] - pos[None, :]
    decay = jnp.where(diff >= 0, jnp.exp(log_gamma[:, None, None] * jnp.maximum(diff, 0.0)), 0.0)
    scores = jnp.einsum('bnihd,bnjhd->bnhij', qc, kc) * decay
    o_inner = jnp.einsum('bnhij,bnjhe->bnihe', scores, vc)
    k_w = jnp.exp(log_gamma[None, :] * (CHUNK - 1.0 - pos)[:, None])
    u = jnp.einsum('bnjhd,jh,bnjhe->nbhde', kc, k_w, vc)
    chunk_decay = jnp.exp(log_gamma * CHUNK)[None, :, None, None]

    def step(s, u_c):
        return chunk_decay * s + u_c, s

    s_fin, s_prev = lax.scan(step, s0, u)
    q_w = jnp.exp(log_gamma[None, :] * (pos + 1.0)[:, None])
    o_cross = jnp.einsum('bnihd,ih,nbhde->bnihe', qc, q_w, s_prev)
    return (o_inner + o_cross).reshape(b, t, h, dv), s_fin


def head_group_norm(o, g, eps=1e-5):
    b, t, h, dv = o.shape
    mu = jnp.mean(o, -1, keepdims=True)
    var = jnp.mean(jnp.square(o - mu), -1, keepdims=True)
    return ((o - mu) * lax.rsqrt(var + eps)).reshape(b, t, h * dv) * g.astype(jnp.float32)


def retention_mixer(pc, pl, pos, ret_decay, g_out):
    f32 = lambda a: a.astype(jnp.float32)

    def heads(a, d):
        return a.reshape(a.shape[0], a.shape[1], RET_HEADS, d)

    ql = axial_rope(heads(pl[0], RET_DK), pos[0], pos[1])
    kl = axial_rope(heads(pl[1], RET_DK), pos[0], pos[1])
    scale = RET_DK ** -0.5
    ctx_seq = (f32(heads(pc[0], RET_DK)), f32(heads(pc[1], RET_DK)) * scale, f32(heads(pc[2], RET_DV)))
    lat_seq = (f32(ql), f32(kl) * scale, f32(heads(pl[2], RET_DV)))
    log_gamma = jnp.log1p(-jnp.exp(ret_decay.astype(jnp.float32)))
    scan_f = lambda q, k, v, s: retention_chunked(q, k, v, log_gamma[0], s)
    scan_b = lambda q, k, v, s: retention_chunked(q, k, v, log_gamma[1], s)
    s0 = jnp.zeros((pl[0].shape[0], RET_HEADS, RET_DK, RET_DV), jnp.float32)
    o_c, o_l = prefix_bidirectional(scan_f, scan_b, ctx_seq, ctx_seq, lat_seq, lat_seq, s0)
    out_c = (head_group_norm(o_c, g_out) * jax.nn.silu(f32(pc[3]))).astype(pc[3].dtype)
    out_l = (head_group_norm(o_l, g_out) * jax.nn.silu(f32(pl[3]))).astype(pl[3].dtype)
    return out_c, out_l


def unit_lower_inverse(a):
    c = a.shape[-1]
    eye = jnp.eye(c, dtype=a.dtype)
    p = -a
    t = eye + p
    for _ in range(int(math.log2(c)) - 1):
        p = p @ p
        t = t @ (eye + p)
    return t


def gated_delta_chunked(q, k, v, log_a, beta, s0):
    b, t, h, dk = q.shape
    dv = v.shape[-1]
    n = t // CHUNK

    def chunks(a):
        return jnp.moveaxis(a.reshape((b, n, CHUNK) + a.shape[2:]), 2, 3)

    qc, kc, vc = chunks(q), chunks(k), chunks(v)
    bc = chunks(beta)
    g = jnp.cumsum(chunks(log_a), axis=-1)
    lower = jnp.tril(jnp.ones((CHUNK, CHUNK), bool))
    strict = jnp.tril(jnp.ones((CHUNK, CHUNK), bool), -1)
    gdiff = g[..., :, None] - g[..., None, :]
    decay = jnp.where(lower, jnp.exp(jnp.where(lower, gdiff, 0.0)), 0.0)
    kb = kc * bc[..., None]
    a_mat = jnp.where(strict, jnp.einsum('bnhid,bnhjd->bnhij', kb, kc) * decay, 0.0)
    tmat = unit_lower_inverse(a_mat)
    u = tmat @ (vc * bc[..., None])
    w = tmat @ (kb * jnp.exp(g)[..., None])
    attn = jnp.where(lower, jnp.einsum('bnhid,bnhjd->bnhij', qc, kc) * decay, 0.0)
    q_in = qc * jnp.exp(g)[..., None]
    g_last = g[..., -1]
    k_tail = kc * jnp.exp(g_last[..., None] - g)[..., None]

    def step(s, xs):
        u_c, w_c, a_c, qi_c, kt_c, gl_c = xs
        v_new = u_c - w_c @ s
        o = qi_c @ s + a_c @ v_new
        s = s * jnp.exp(gl_c)[..., None, None] + jnp.swapaxes(kt_c, -1, -2) @ v_new
        return s, o

    xs = tuple(jnp.moveaxis(a, 1, 0) for a in (u, w, attn, q_in, k_tail, g_last))
    s_fin, o = lax.scan(step, s0, xs)
    o = jnp.moveaxis(jnp.moveaxis(o, 0, 1), 3, 2).reshape(b, t, h, dv)
    return o, s_fin


def gdn_prepare(q, k, v, a, bg, conv_w, a_log, dt_bias):
    bsz, t = q.shape[:2]
    f32 = jnp.float32
    qkv = jax.nn.silu(depthwise_conv(jnp.concatenate([q, k, v], -1), conv_w)).astype(f32)
    nq = GDN_HEADS * GDN_DK
    qh = l2norm(qkv[..., :nq].reshape(bsz, t, GDN_HEADS, GDN_DK)) * (GDN_DK ** -0.5)
    kh = l2norm(qkv[..., nq:2 * nq].reshape(bsz, t, GDN_HEADS, GDN_DK))
    vh = qkv[..., 2 * nq:].reshape(bsz, t, GDN_HEADS, GDN_DV)
    log_a = -jnp.exp(a_log.astype(f32)) * jax.nn.softplus(a.astype(f32).reshape(bsz, t, 2, GDN_HEADS) + dt_bias.astype(f32))
    beta = jax.nn.sigmoid(bg.astype(f32).reshape(bsz, t, 2, GDN_HEADS))
    fwd = (qh, kh, vh, log_a[:, :, 0], beta[:, :, 0])
    bwd = (qh, kh, vh, log_a[:, :, 1], beta[:, :, 1])
    return fwd, bwd


def gdn_mixer(pc, pl, conv_w, a_log, dt_bias, g_out):
    cf, cb = gdn_prepare(pc[0], pc[1], pc[2], pc[4], pc[5], conv_w, a_log, dt_bias)
    lf, lb = gdn_prepare(pl[0], pl[1], pl[2], pl[4], pl[5], conv_w, a_log, dt_bias)
    s0 = jnp.zeros((pl[0].shape[0], GDN_HEADS, GDN_DK, GDN_DV), jnp.float32)
    o_c, o_l = prefix_bidirectional(gated_delta_chunked, gated_delta_chunked, cf, cb, lf, lb, s0)

    def gated_norm(o, z):
        bsz, t = o.shape[:2]
        zf = z.astype(jnp.float32).reshape(bsz, t, GDN_HEADS, GDN_DV)
        y = o * lax.rsqrt(jnp.mean(o * o, -1, keepdims=True) + NORM_EPS) * g_out.astype(jnp.float32) * jax.nn.silu(zf)
        return y.reshape(bsz, t, GDN_W).astype(z.dtype)

    return gated_norm(o_c, pc[3]), gated_norm(o_l, pl[3])


def token_mixer(h_ctx, h_lat, pos, w_in, g_cq, g_ckv, w_uq, w_ukv, g_mla, ret_decay, g_ret,
                gdn_conv, gdn_a_log, gdn_dt_bias, g_gdn, w_out):
    pc = split_columns(h_ctx @ w_in)
    pl = split_columns(h_lat @ w_in)
    mc, ml = mla_mixer(pc[0:3], pl[0:3], pos, g_cq, g_ckv, w_uq, w_ukv, g_mla)
    rc, rl = retention_mixer(pc[3:7], pl[3:7], pos, ret_decay, g_ret)
    gc, gl = gdn_mixer(pc[7:13], pl[7:13], gdn_conv, gdn_a_log, gdn_dt_bias, g_gdn)
    o_ctx = jnp.concatenate([mc, rc, gc], -1) @ w_out
    o_lat = jnp.concatenate([ml, rl, gl], -1) @ w_out
    return o_ctx, o_lat


def moe_ffn(h, w_router, b_router, w_gu, b_gu, w_dn, b_dn):
    n_tok, d = h.shape
    e_n = w_router.shape[-1]
    logits = (h @ w_router).astype(jnp.float32) + b_router.astype(jnp.float32)
    top_val, top_idx = lax.top_k(logits, TOP_K)
    gate_w = jax.nn.softmax(top_val, axis=-1).astype(h.dtype)
    nk = n_tok * TOP_K
    e_flat = top_idx.reshape(-1)
    tok_flat = jnp.arange(nk, dtype=jnp.int32) // TOP_K
    order = jnp.argsort(e_flat)
    e_sorted = e_flat[order]
    counts = jnp.zeros((e_n,), jnp.int32).at[e_flat].add(1)
    padded = (counts + MOE_BLOCK - 1) // MOE_BLOCK * MOE_BLOCK
    start = jnp.cumsum(counts) - counts
    pend = jnp.cumsum(padded)
    pstart = pend - padded
    dest_sorted = pstart[e_sorted] + (jnp.arange(nk, dtype=jnp.int32) - start[e_sorted])
    n_blocks = -(-nk // MOE_BLOCK) + e_n
    rows_p = n_blocks * MOE_BLOCK
    row_tok = jnp.full((rows_p,), n_tok, jnp.int32).at[dest_sorted].set(tok_flat[order])
    h_pad = jnp.concatenate([h, jnp.zeros((1, d), h.dtype)], 0)
    xs = h_pad[row_tok].reshape(n_blocks, MOE_BLOCK, d)
    block_e = jnp.minimum(jnp.searchsorted(pend, jnp.arange(n_blocks, dtype=jnp.int32) * MOE_BLOCK, side='right'), e_n - 1)

    def expert_block(args):
        xb, e = args
        gu = xb @ w_gu[e] + b_gu[e]
        gate = jnp.minimum(gu[:, :D_EXPERT], SWIGLU_LIMIT)
        up = jnp.clip(gu[:, D_EXPERT:], -SWIGLU_LIMIT, SWIGLU_LIMIT)
        act = (up + 1.0) * (gate * jax.nn.sigmoid(gate * SWIGLU_ALPHA))
        return act @ w_dn[e] + b_dn[e]

    ys = lax.map(expert_block, (xs, block_e)).reshape(rows_p, d)
    dest_slot = jnp.zeros((nk,), jnp.int32).at[order].set(dest_sorted)
    y = ys[dest_slot].reshape(n_tok, TOP_K, d)
    return jnp.einsum('nkd,nk->nd', y, gate_w)


def setup_inputs(seed: int = 0) -> dict:
    key = jax.random.key(seed)
    ks = jax.random.split(key, 28)
    f32 = jnp.float32
    nrm = lambda k, shape, scale: jax.random.normal(k, shape, f32) * scale
    gain = lambda k, shape: 1.0 + 0.02 * jax.random.normal(k, shape, f32)
    L, D, E, F = DEPTH, D_MODEL, N_EXPERTS, D_EXPERT
    ret_base = -math.log(2.0) * (5.0 + jnp.arange(RET_HEADS, dtype=f32))
    dt = jnp.exp(jax.random.uniform(ks[18], (L, 2, GDN_HEADS), f32, math.log(1e-3), math.log(1e-1)))
    return {
        'x': nrm(ks[0], (BATCH, SEQ, D), 1.0),
        'c': nrm(ks[1], (BATCH, D), 1.0),
        'ctx': nrm(ks[2], (BATCH, CTX_LEN, D), 1.0),
        'c_ctx': nrm(ks[3], (D,), 1.0),
        'w_mod': nrm(ks[4], (L, D, 6 * D), 0.5 * D ** -0.5),
        'b_mod': nrm(ks[5], (L, 6 * D), 0.02),
        'g_norm1': gain(ks[6], (L, D)),
        'g_norm2': gain(ks[7], (L, D)),
        'w_in': nrm(ks[8], (L, D, IN_W), D ** -0.5),
        'g_cq': gain(ks[9], (L, MLA_Q_LORA)),
        'g_ckv': gain(ks[10], (L, MLA_KV_LORA)),
        'w_uq': nrm(ks[11], (L, MLA_Q_LORA, MLA_HEADS * (MLA_NOPE + MLA_ROPE)), MLA_Q_LORA ** -0.5),
        'w_ukv': nrm(ks[12], (L, MLA_KV_LORA, MLA_HEADS * (MLA_NOPE + MLA_V)), MLA_KV_LORA ** -0.5),
        'g_mla': gain(ks[13], (L, MLA_W)),
        'ret_decay': ret_base + nrm(ks[14], (L, 2, RET_HEADS), 0.1),
        'g_ret': gain(ks[15], (L, RET_W)),
        'gdn_conv': nrm(ks[16], (L, GDN_CONV, GDN_CONV_CH), GDN_CONV ** -0.5),
        'gdn_a_log': jnp.log(jax.random.uniform(ks[17], (L, 2, GDN_HEADS), f32, 1.0, 16.0)),
        'gdn_dt_bias': dt + jnp.log(-jnp.expm1(-dt)),
        'g_gdn': gain(ks[19], (L, GDN_DV)),
        'w_out': nrm(ks[20], (L, MIX_W, D), MIX_W ** -0.5),
        'w_router': nrm(ks[21], (L, D, E), D ** -0.5),
        'b_router': nrm(ks[22], (L, E), 0.01),
        'w_gate_up': nrm(ks[23], (L, E, D, 2 * F), D ** -0.5),
        'b_gate_up': nrm(ks[24], (L, E, 2 * F), 0.01),
        'w_down': nrm(ks[25], (L, E, F, D), F ** -0.5),
        'b_down': nrm(ks[26], (L, E, D), 0.01),
        'g_final': gain(ks[27], (D,)),
    }


def reference(x, c, ctx, c_ctx, w_mod, b_mod, g_norm1, g_norm2, w_in, g_cq, g_ckv, w_uq, w_ukv, g_mla,
              ret_decay, g_ret, gdn_conv, gdn_a_log, gdn_dt_bias, g_gdn, w_out, w_router, b_router,
              w_gate_up, b_gate_up, w_down, b_down, g_final):
    b, s, d = x.shape
    grid_rows = s // GRID_W
    rows = jnp.repeat(jnp.arange(grid_rows, dtype=jnp.int32), GRID_W)
    cols = jnp.tile(jnp.arange(GRID_W, dtype=jnp.int32), grid_rows)
    pos = (rows, cols)
    xc = ctx
    n_ctx_tok = b * ctx.shape[1]
    for l in range(DEPTH):
        last = l == DEPTH - 1
        mod_l = jax.nn.silu(c) @ w_mod[l] + b_mod[l]
        mod_c = jax.nn.silu(c_ctx) @ w_mod[l] + b_mod[l]
        sh1, sc1, gt1, sh2, sc2, gt2 = jnp.split(mod_l[:, None, :], 6, axis=-1)
        csh1, csc1, cgt1, csh2, csc2, cgt2 = jnp.split(mod_c, 6, axis=-1)
        h_l = rms_norm(x, g_norm1[l]) * (1.0 + sc1) + sh1
        h_c = rms_norm(xc, g_norm1[l]) * (1.0 + csc1) + csh1
        o_c, o_l = token_mixer(h_c, h_l, pos, w_in[l], g_cq[l], g_ckv[l], w_uq[l], w_ukv[l], g_mla[l],
                               ret_decay[l], g_ret[l], gdn_conv[l], gdn_a_log[l], gdn_dt_bias[l], g_gdn[l], w_out[l])
        x = x + gt1 * o_l
        h_l = rms_norm(x, g_norm2[l]) * (1.0 + sc2) + sh2
        if last:
            y = moe_ffn(h_l.reshape(-1, d), w_router[l], b_router[l], w_gate_up[l], b_gate_up[l], w_down[l], b_down[l])
            x = x + gt2 * y.reshape(b, s, d)
        else:
            xc = xc + cgt1 * o_c
            h_c = rms_norm(xc, g_norm2[l]) * (1.0 + csc2) + csh2
            tokens = jnp.concatenate([h_c.reshape(-1, d), h_l.reshape(-1, d)], 0)
            y = moe_ffn(tokens, w_router[l], b_router[l], w_gate_up[l], b_gate_up[l], w_down[l], b_down[l])
            xc = xc + cgt2 * y[:n_ctx_tok].reshape(xc.shape)
            x = x + gt2 * y[n_ctx_tok:].reshape(b, s, d)
    return rms_norm(x, g_final)
```

```python
import functools
import math

import jax
import jax.numpy as jnp
from jax import lax
from jax.experimental import pallas as pl
from jax.experimental.pallas import tpu as pltpu

F32 = jnp.float32
MXU_DTYPE = jnp.bfloat16

LANE = 128
V7X_VMEM_LIMIT_BYTES = 56 * 1024 * 1024

MLA_HEADS = 4
MLA_NOPE = 128
MLA_ROPE = 64
MLA_V = 128
MLA_Q_LORA = 256
MLA_KV_LORA = 128
MLA_QK = 256
RET_HEADS = 4
RET_DK = 128
GDN_HEADS = 4
GDN_DK = 128
GDN_CONV = 5
GDN_CHUNK = 64
RET_CHUNK = 128
TOP_K = 4
SWIGLU_LIMIT = 7.0
SWIGLU_ALPHA = 1.702
MOE_BLOCK = 256
GRID_W = 64
ROPE_BASE = 10000.0
NORM_EPS = 1e-6
GROUP_NORM_EPS = 1e-5
L2_EPS = 1e-6
ROW_BLOCK = 256

PB_DQ = 0
PB_DKV = 2
PB_RET_Q, PB_RET_K, PB_RET_V, PB_RET_G = 4, 8, 12, 16
PB_GDN_Q, PB_GDN_K, PB_GDN_V, PB_GDN_Z = 20, 24, 28, 32
PB_GDN_AB = 36
P_BLOCKS = 37


def _cparams(*sem):
    return pltpu.CompilerParams(dimension_semantics=sem, vmem_limit_bytes=V7X_VMEM_LIMIT_BYTES)


def _mm(a, b):
    return jnp.dot(a.astype(MXU_DTYPE), b.astype(MXU_DTYPE), preferred_element_type=F32)


def _mm_nt(a, b):
    return lax.dot_general(a.astype(MXU_DTYPE), b.astype(MXU_DTYPE), (((1,), (1,)), ((), ())),
                           preferred_element_type=F32)


def _mm_tn(a, b):
    return jnp.dot(a.T.astype(MXU_DTYPE), b.astype(MXU_DTYPE), preferred_element_type=F32)


def _rms(x, g, eps=NORM_EPS):
    return x * lax.rsqrt(jnp.mean(x * x, -1, keepdims=True) + eps) * g


def _silu(x):
    return x * jax.nn.sigmoid(x)


def _swap_pairs(x, q):
    n = x.shape[-1]
    lane = lax.broadcasted_iota(jnp.int32, x.shape, x.ndim - 1)
    first = (lane % (2 * q)) < q
    return jnp.where(first, pltpu.roll(x, n - q, x.ndim - 1), pltpu.roll(x, q, x.ndim - 1))


def _mod_kernel(c_ref, w_ref, b_ref, o_ref):
    o_ref[0] = _mm(_silu(c_ref[...]), w_ref[0]) + b_ref[0]


def _modulation(cs, w_mod, b_mod):
    depth, d, n = w_mod.shape
    tn = n // 6
    return pl.pallas_call(
        _mod_kernel,
        grid=(depth, n // tn),
        in_specs=[pl.BlockSpec((8, d), lambda l, j: (0, 0)),
                  pl.BlockSpec((1, d, tn), lambda l, j: (l, 0, j)),
                  pl.BlockSpec((1, 1, tn), lambda l, j: (l, 0, j))],
        out_specs=pl.BlockSpec((1, 8, tn), lambda l, j: (l, 0, j)),
        out_shape=jax.ShapeDtypeStruct((depth, 8, n), F32),
        compiler_params=_cparams("parallel", "parallel"),
        name="modulation",
    )(cs, w_mod, b_mod.reshape(depth, 1, n))


def _inproj_kernel(has_y, *refs):
    if has_y:
        x_ref, y_ref, gt_ref, g_ref, sc_ref, sh_ref, w_ref, xo_ref, p_ref = refs
        x = x_ref[...] + gt_ref[0] * y_ref[...]
        xo_ref[...] = x
    else:
        x_ref, g_ref, sc_ref, sh_ref, w_ref, p_ref = refs
        x = x_ref[...]
    h = _rms(x, g_ref[...]) * (1.0 + sc_ref[0]) + sh_ref[0]
    p_ref[...] = _mm(h, w_ref[...])


def _mod_spec(dims, chunk):
    d = dims["D"]
    return pl.BlockSpec((1, 1, d), lambda i: (_mod_index(dims, i), 0, chunk))


def _mod_index(dims, i):
    nb = dims["L"] // dims["RB"]
    return jnp.where(i % nb < dims["CT"] // dims["RB"], dims["B"], i // nb)


def _inproj(dims, x, y, mod_l, mod_prev, g1, w_in_p):
    t, d, rb = dims["T"], dims["D"], dims["RB"]
    np_ = w_in_p.shape[1]
    row = pl.BlockSpec((rb, d), lambda i: (i, 0))
    vec = pl.BlockSpec((1, d), lambda i: (0, 0))
    wspec = pl.BlockSpec((d, np_), lambda i: (0, 0))
    pspec = pl.BlockSpec((rb, np_), lambda i: (i, 0))
    pshape = jax.ShapeDtypeStruct((t, np_), F32)
    if y is None:
        return x, pl.pallas_call(
            functools.partial(_inproj_kernel, False),
            grid=(t // rb,),
            in_specs=[row, vec, _mod_spec(dims, 1), _mod_spec(dims, 0), wspec],
            out_specs=pspec, out_shape=pshape,
            compiler_params=_cparams("parallel"), name="inproj",
        )(x, g1, mod_l, mod_l, w_in_p)
    return pl.pallas_call(
        functools.partial(_inproj_kernel, True),
        grid=(t // rb,),
        in_specs=[row, row, _mod_spec(dims, 5), vec, _mod_spec(dims, 1), _mod_spec(dims, 0), wspec],
        out_specs=[row, pspec],
        out_shape=[jax.ShapeDtypeStruct((t, d), F32), pshape],
        compiler_params=_cparams("parallel"), name="inproj_res",
    )(x, y, mod_prev, g1, mod_l, mod_l, w_in_p)


def _mla_prep_kernel(dq_ref, x2_ref, gq_ref, gkv_ref, wq_ref, wk_ref, wv_ref, cos_ref, sin_ref,
                     q_ref, k_ref, v_ref):
    cos = cos_ref[...]
    sin = sin_ref[...]

    def rope(r):
        return r * cos + _swap_pairs(r, MLA_ROPE // 4) * sin

    q = _mm(_rms(dq_ref[...], gq_ref[...]), wq_ref[...])
    x2 = x2_ref[...]
    kvn = _rms(x2[:, :MLA_KV_LORA], gkv_ref[...])
    kn = _mm(kvn, wk_ref[...])
    v_ref[...] = _mm(kvn, wv_ref[...]).astype(v_ref.dtype)
    kr = rope(x2[:, MLA_KV_LORA:]).astype(k_ref.dtype)
    for h in range(MLA_HEADS):
        o = h * MLA_QK
        q_ref[:, o:o + MLA_NOPE] = q[:, o:o + MLA_NOPE].astype(q_ref.dtype)
        q_ref[:, o + MLA_NOPE:o + MLA_QK] = rope(q[:, o + MLA_NOPE:o + MLA_QK]).astype(q_ref.dtype)
        k_ref[:, o:o + MLA_NOPE] = kn[:, h * MLA_NOPE:(h + 1) * MLA_NOPE].astype(k_ref.dtype)
        k_ref[:, o + MLA_NOPE:o + MLA_QK] = kr


def _pos_block(dims, i):
    nb = dims["L"] // dims["RB"]
    nc = dims["CT"] // dims["RB"]
    w = i % nb
    return jnp.where(w < nc, dims["S"] // dims["RB"], w - nc)


def _mla_prep(dims, p, g_cq, g_ckv, wq, wk, wv, cos_t, sin_t):
    t, rb = dims["T"], dims["RB"]
    hq = MLA_HEADS * MLA_QK
    hv = MLA_HEADS * MLA_V
    full = lambda a: pl.BlockSpec(a.shape, lambda i: (0, 0))
    tab = pl.BlockSpec((rb, LANE), lambda i: (_pos_block(dims, i), 0))
    return pl.pallas_call(
        _mla_prep_kernel,
        grid=(t // rb,),
        in_specs=[pl.BlockSpec((rb, 2 * LANE), lambda i: (i, 0)),
                  pl.BlockSpec((rb, 2 * LANE), lambda i: (i, 1)),
                  full(g_cq), full(g_ckv), full(wq), full(wk), full(wv), tab, tab],
        out_specs=[pl.BlockSpec((rb, hq), lambda i: (i, 0)),
                   pl.BlockSpec((rb, hq), lambda i: (i, 0)),
                   pl.BlockSpec((rb, hv), lambda i: (i, 0))],
        out_shape=[jax.ShapeDtypeStruct((t, hq), MXU_DTYPE),
                   jax.ShapeDtypeStruct((t, hq), MXU_DTYPE),
                   jax.ShapeDtypeStruct((t, hv), MXU_DTYPE)],
        compiler_params=_cparams("parallel"), name="mla_prep",
    )(p, p, g_cq, g_ckv, wq, wk, wv, cos_t, sin_t)


def _attn_kernel(n_ctx_blocks, ctx_len, q_ref, k_ref, v_ref, o_ref):
    scale = (MLA_NOPE + MLA_ROPE) ** -0.5

    def attend(nk):
        s = _mm_nt(q_ref[...], k_ref[0:nk, :]) * scale
        m = jnp.max(s, -1, keepdims=True)
        e = jnp.exp(s - m)
        den = jnp.sum(e, -1, keepdims=True)
        o_ref[...] = _mm(e, v_ref[0:nk, :]) / den

    i = pl.program_id(2)

    @pl.when(i < n_ctx_blocks)
    def _():
        attend(ctx_len)

    @pl.when(i >= n_ctx_blocks)
    def _():
        attend(k_ref.shape[0])


def _attention(dims, q, k, v):
    b, l, t, rb, ct = dims["B"], dims["L"], dims["T"], dims["RB"], dims["CT"]
    nq = l // rb
    return pl.pallas_call(
        functools.partial(_attn_kernel, ct // rb, ct),
        grid=(b, MLA_HEADS, nq),
        in_specs=[pl.BlockSpec((rb, MLA_QK), lambda bi, h, i: (bi * nq + i, h)),
                  pl.BlockSpec((l, MLA_QK), lambda bi, h, i: (bi, h)),
                  pl.BlockSpec((l, MLA_V), lambda bi, h, i: (bi, h))],
        out_specs=pl.BlockSpec((rb, MLA_V), lambda bi, h, i: (bi * nq + i, h)),
        out_shape=jax.ShapeDtypeStruct((t, MLA_HEADS * MLA_V), F32),
        compiler_params=_cparams("parallel", "parallel", "parallel"), name="mla_attention",
    )(q, k, v)


def _retention_kernel(n_ctx, n_all, q_ref, k_ref, v_ref, gate_ref, cos_ref, sin_ref, dm_ref,
                      wkf_ref, wqf_ref, wkb_ref, wqb_ref, cd_ref, g_ref, o_ref, sf_ref, kr_ref):
    c = RET_CHUNK
    scale = RET_DK ** -0.5
    cd = cd_ref[0]
    decay_f = cd[0:1, :]
    decay_b = cd[1:2, :]

    def rows(n):
        return pl.ds(pl.multiple_of(n * c, c), c)

    def rope(x, n):
        r = pl.ds(pl.multiple_of((n - n_ctx) * c, c), c)
        return x * cos_ref[r, :] + _swap_pairs(x, RET_DK // 4) * sin_ref[r, :]

    def fwd_step(n, s, latent):
        k = k_ref[rows(n), :]
        if latent:
            k = rope(k, n)
        k = k * scale
        kr_ref[rows(n), :] = k
        sf_ref[n] = s
        return decay_f * s + _mm_tn(k * wkf_ref[0], v_ref[rows(n), :])

    s = jnp.zeros((RET_DK, LANE), F32)
    for n in range(n_ctx):
        s = fwd_step(n, s, False)
    lax.fori_loop(n_ctx, n_all, lambda n, s: fwd_step(n, s, True), s)

    def out_step(n, sb, latent):
        q = q_ref[rows(n), :]
        if latent:
            q = rope(q, n)
        k = kr_ref[rows(n), :]
        v = v_ref[rows(n), :]
        scores = _mm_nt(q, k) * dm_ref[0]
        o = _mm(scores, v) + wqf_ref[0] * _mm(q, sf_ref[n]) + wqb_ref[0] * _mm(q, sb)
        mu = jnp.mean(o, -1, keepdims=True)
        oc = o - mu
        var = jnp.mean(oc * oc, -1, keepdims=True)
        y = oc * lax.rsqrt(var + GROUP_NORM_EPS) * g_ref[...]
        o_ref[rows(n), :] = y * _silu(gate_ref[rows(n), :])
        return decay_b * sb + _mm_tn(k * wkb_ref[0], v)

    sb = jnp.zeros((RET_DK, LANE), F32)
    for n in reversed(range(n_ctx)):
        sb = out_step(n, sb, False)
    lax.fori_loop(0, n_all - n_ctx, lambda j, sb: out_step(n_all - 1 - j, sb, True), sb)


def _retention_tables(ret_decay):
    c = RET_CHUNK
    lg = jnp.log1p(-jnp.exp(ret_decay.astype(F32)))
    pos = jnp.arange(c, dtype=F32)
    diff = pos[:, None] - pos[None, :]
    lf = lg[0][:, None, None]
    lb = lg[1][:, None, None]
    dm = jnp.where(diff > 0, jnp.exp(lf * jnp.maximum(diff, 0.0)), 0.0) \
        + jnp.where(diff < 0, jnp.exp(lb * jnp.maximum(-diff, 0.0)), 0.0) \
        + jnp.where(diff == 0, 2.0, 0.0)
    bc = lambda a: jnp.broadcast_to(a[:, :, None], (RET_HEADS, c, LANE))
    wkf = bc(jnp.exp(lg[0][:, None] * (c - 1.0 - pos)[None, :]))
    wqf = bc(jnp.exp(lg[0][:, None] * (pos + 1.0)[None, :]))
    wkb = bc(jnp.exp(lg[1][:, None] * pos[None, :]))
    wqb = bc(jnp.exp(lg[1][:, None] * (c - pos)[None, :]))
    cd = jnp.zeros((RET_HEADS, 8, LANE), F32)
    cd = cd.at[:, 0, :].set(jnp.exp(lg[0] * c)[:, None]).at[:, 1, :].set(jnp.exp(lg[1] * c)[:, None])
    return dm, wkf, wqf, wkb, wqb, cd


def _retention(dims, p, cos_t, sin_t, tables, g_ret):
    b, l, t, s, ct = dims["B"], dims["L"], dims["T"], dims["S"], dims["CT"]
    c = RET_CHUNK
    dm, wkf, wqf, wkb, wqb, cd = tables
    col = lambda base: pl.BlockSpec((l, LANE), lambda bi, h: (bi, base + h))
    tab = pl.BlockSpec((s, LANE), lambda bi, h: (0, 0))
    per_head = lambda a: pl.BlockSpec((1,) + a.shape[1:], lambda bi, h: (h, 0, 0))
    return pl.pallas_call(
        functools.partial(_retention_kernel, ct // c, l // c),
        grid=(b, RET_HEADS),
        in_specs=[col(PB_RET_Q), col(PB_RET_K), col(PB_RET_V), col(PB_RET_G), tab, tab,
                  per_head(dm), per_head(wkf), per_head(wqf), per_head(wkb), per_head(wqb), per_head(cd),
                  pl.BlockSpec((1, LANE), lambda bi, h: (0, h))],
        out_specs=pl.BlockSpec((l, LANE), lambda bi, h: (bi, h)),
        out_shape=jax.ShapeDtypeStruct((t, RET_HEADS * LANE), F32),
        scratch_shapes=[pltpu.VMEM((l // c, RET_DK, LANE), F32), pltpu.VMEM((l, LANE), F32)],
        compiler_params=_cparams("parallel", "parallel"), name="retention",
    )(p, p, p, p, cos_t, sin_t, dm, wkf, wqf, wkb, wqb, cd, g_ret)


def _gdn_kernel(ctx_len, q_ref, k_ref, v_ref, z_ref, ab_ref, wq_ref, wk_ref, wv_ref, gp_ref, g_ref,
                o_ref, qn_ref, kn_ref, vn_ref, cp_ref, cs_ref, bt_ref):
    l = q_ref.shape[0]
    blk = 2 * GDN_CHUNK
    n_blk = l // blk
    n_ctx = ctx_len // blk
    h = pl.program_id(1)

    def conv_segment(src, w_ref, dst, r0, n, mode):
        x = src[r0:r0 + n, :]
        w = w_ref[0]
        row = lax.broadcasted_iota(jnp.int32, x.shape, 0)
        acc = x * w[GDN_CONV // 2:GDN_CONV // 2 + 1, :]
        for d in range(1, GDN_CONV // 2 + 1):
            back = jnp.where(row >= d, pltpu.roll(x, d, 0), 0.0)
            fwd = jnp.where(row < n - d, pltpu.roll(x, n - d, 0), 0.0)
            acc = acc + back * w[GDN_CONV // 2 - d:GDN_CONV // 2 - d + 1, :] \
                + fwd * w[GDN_CONV // 2 + d:GDN_CONV // 2 + d + 1, :]
        y = _silu(acc)
        if mode != "v":
            y = y * lax.rsqrt(jnp.sum(y * y, -1, keepdims=True) + L2_EPS)
        if mode == "q":
            y = y * (GDN_DK ** -0.5)
        dst[r0:r0 + n, :] = y

    for src, w_ref, dst, mode in ((q_ref, wq_ref, qn_ref, "q"), (k_ref, wk_ref, kn_ref, "k"),
                                  (v_ref, wv_ref, vn_ref, "v")):
        conv_segment(src, w_ref, dst, 0, ctx_len, mode)
        conv_segment(src, w_ref, dst, ctx_len, l - ctx_len, mode)

    ab = ab_ref[...]
    gp = gp_ref[...]
    sp = ab + gp[1:2, :]
    la = -jnp.exp(gp[0:1, :]) * (jnp.maximum(sp, 0.0) + jnp.log1p(jnp.exp(-jnp.abs(sp))))
    bt_ref[...] = jax.nn.sigmoid(ab)
    pos = lax.broadcasted_iota(jnp.int32, ab.shape, 0) % GDN_CHUNK
    pre = la
    suf = la
    sh = 1
    while sh < GDN_CHUNK:
        pre = pre + jnp.where(pos >= sh, pltpu.roll(pre, sh, 0), 0.0)
        suf = suf + jnp.where(pos < GDN_CHUNK - sh, pltpu.roll(suf, l - sh, 0), 0.0)
        sh *= 2
    cp_ref[...] = pre
    cs_ref[...] = suf

    o_ref[...] = jnp.zeros(o_ref.shape, F32)

    ri = lax.broadcasted_iota(jnp.int32, (blk, blk), 0)
    ci = lax.broadcasted_iota(jnp.int32, (blk, blk), 1)
    same = (ri // GDN_CHUNK) == (ci // GDN_CHUNK)
    eye = jnp.where(ri == ci, 1.0, 0.0)
    lane = lax.broadcasted_iota(jnp.int32, (blk, LANE), 1)
    row_lo = lax.broadcasted_iota(jnp.int32, (blk, LANE), 0) < GDN_CHUNK

    def pick(x, c):
        return jnp.broadcast_to(jnp.sum(jnp.where(lane == c, x, 0.0), -1, keepdims=True), (blk, LANE))

    def chain_block(n, s, backward):
        r = pl.ds(pl.multiple_of(n * blk, blk), blk)
        q = qn_ref[r, :]
        k = kn_ref[r, :]
        v = vn_ref[r, :]
        d = 1 if backward else 0
        gb = pick((cs_ref if backward else cp_ref)[r, :], d * GDN_HEADS + h)
        beta = pick(bt_ref[r, :], 2 * GDN_HEADS + d * GDN_HEADS + h)
        gdiff = gb - gb.T
        incl = same & ((ci >= ri) if backward else (ci <= ri))
        strict = same & ((ci > ri) if backward else (ci < ri))
        decay = jnp.where(incl, jnp.exp(jnp.where(incl, gdiff, 0.0)), 0.0)
        kb = k * beta
        a = jnp.where(strict, _mm_nt(kb, k) * decay, 0.0)
        pw = -a
        tm = eye + pw
        for _ in range(int(math.log2(GDN_CHUNK)) - 1):
            pw = _mm(pw, pw)
            tm = _mm(tm, eye + pw)
        eg = jnp.exp(gb)
        u = _mm(tm, v * beta)
        w = _mm(tm, kb * eg)
        attn = _mm_nt(q, k) * decay
        q_in = q * eg
        e0, e1 = (0, GDN_CHUNK) if backward else (GDN_CHUNK - 1, blk - 1)
        g_end = jnp.where(row_lo, gb[e0:e0 + 1, :], gb[e1:e1 + 1, :])
        k_tail = k * jnp.exp(g_end - gb)
        outs = [None, None]
        for c in ((1, 0) if backward else (0, 1)):
            rc = slice(c * GDN_CHUNK, (c + 1) * GDN_CHUNK)
            in_c = row_lo if c == 0 else jnp.logical_not(row_lo)
            v_new = u[rc, :] - _mm(w[rc, :], s)
            zero = jnp.zeros_like(v_new)
            vn_full = jnp.concatenate([v_new, zero] if c == 0 else [zero, v_new], 0)
            outs[c] = _mm(q_in[rc, :], s) + _mm(attn[rc, :], vn_full)
            e = e0 if c == 0 else e1
            s = s * jnp.exp(gb[e:e + 1, :]) + _mm_tn(jnp.where(in_c, k_tail, 0.0), vn_full)
        o_ref[r, :] = o_ref[r, :] + jnp.concatenate(outs, 0)
        return s

    def body(t, carry):
        sf, sb = carry
        sf = chain_block(t, sf, False)
        nb = jnp.where(t < n_ctx, n_ctx - 1 - t, n_blk - 1 + n_ctx - t)
        sb = chain_block(nb, sb, True)
        return sf, sb

    zero_state = jnp.zeros((GDN_DK, LANE), F32)
    lax.fori_loop(0, n_blk, body, (zero_state, zero_state))

    o = o_ref[...]
    y = o * lax.rsqrt(jnp.mean(o * o, -1, keepdims=True) + NORM_EPS) * g_ref[...]
    o_ref[...] = y * _silu(z_ref[...])


def _gdn(dims, p, conv_w, gate_params, g_gdn):
    b, l, t, ct = dims["B"], dims["L"], dims["T"], dims["CT"]
    col = lambda base: pl.BlockSpec((l, LANE), lambda bi, h: (bi, base + h))
    cw = lambda base: pl.BlockSpec((1, 8, LANE), lambda bi, h: (base + h, 0, 0))
    seq = pltpu.VMEM((l, LANE), F32)
    return pl.pallas_call(
        functools.partial(_gdn_kernel, ct),
        grid=(b, GDN_HEADS),
        in_specs=[col(PB_GDN_Q), col(PB_GDN_K), col(PB_GDN_V), col(PB_GDN_Z),
                  pl.BlockSpec((l, LANE), lambda bi, h: (bi, PB_GDN_AB)),
                  cw(0), cw(GDN_HEADS), cw(2 * GDN_HEADS),
                  pl.BlockSpec((8, LANE), lambda bi, h: (0, 0)),
                  pl.BlockSpec((1, LANE), lambda bi, h: (0, 0))],
        out_specs=pl.BlockSpec((l, LANE), lambda bi, h: (bi, h)),
        out_shape=jax.ShapeDtypeStruct((t, GDN_HEADS * LANE), F32),
        scratch_shapes=[seq] * 6,
        compiler_params=_cparams("parallel", "parallel"), name="gdn",
    )(p, p, p, p, p, conv_w, conv_w, conv_w, gate_params, g_gdn)


def _outproj_kernel(n_experts, om_ref, r_ref, gd_ref, x_ref, gm_ref, w_ref, gt_ref, g2_ref, sc_ref, sh_ref,
                    wr_ref, br_ref, xo_ref, h_ref, ti_ref, tw_ref):
    mixed = jnp.concatenate([_rms(om_ref[...], gm_ref[...]), r_ref[...], gd_ref[...]], -1)
    x = x_ref[...] + gt_ref[0] * _mm(mixed, w_ref[...])
    xo_ref[...] = x
    h = (_rms(x, g2_ref[...]) * (1.0 + sc_ref[0]) + sh_ref[0]).astype(h_ref.dtype)
    h_ref[...] = h
    logits = _mm(h, wr_ref[...]) + br_ref[...]
    lane = lax.broadcasted_iota(jnp.int32, logits.shape, 1)
    neg = jnp.float32(-jnp.inf)
    logits = jnp.where(lane < n_experts, logits, neg)
    vals, idxs = [], []
    for _ in range(TOP_K):
        m = jnp.max(logits, -1, keepdims=True)
        idx = jnp.min(jnp.where(logits == m, lane.astype(F32), float(LANE)), -1, keepdims=True).astype(jnp.int32)
        logits = jnp.where(lane == idx, neg, logits)
        vals.append(m)
        idxs.append(idx)
    es = [jnp.exp(v - vals[0]) for v in vals]
    den = es[0]
    for e in es[1:]:
        den = den + e
    ti = jnp.zeros(logits.shape, jnp.int32)
    tw = jnp.zeros(logits.shape, F32)
    for k in range(TOP_K):
        ti = jnp.where(lane == k, idxs[k], ti)
        tw = jnp.where(lane == k, es[k] / den, tw)
    ti_ref[...] = ti
    tw_ref[...] = tw


def _outproj(dims, om, r, gd, x, mod_l, g_mla, w_out, g2, w_router_p, b_router_p, n_experts):
    t, d, rb = dims["T"], dims["D"], dims["RB"]
    row = lambda w: pl.BlockSpec((rb, w), lambda i: (i, 0))
    full = lambda a: pl.BlockSpec(a.shape, lambda i: (0, 0))
    return pl.pallas_call(
        functools.partial(_outproj_kernel, n_experts),
        grid=(t // rb,),
        in_specs=[row(om.shape[1]), row(r.shape[1]), row(gd.shape[1]), row(d), full(g_mla), full(w_out),
                  _mod_spec(dims, 2), full(g2), _mod_spec(dims, 4), _mod_spec(dims, 3),
                  full(w_router_p), full(b_router_p)],
        out_specs=[row(d), row(d), row(LANE), row(LANE)],
        out_shape=[jax.ShapeDtypeStruct((t, d), F32), jax.ShapeDtypeStruct((t, d), MXU_DTYPE),
                   jax.ShapeDtypeStruct((t, LANE), jnp.int32), jax.ShapeDtypeStruct((t, LANE), F32)],
        compiler_params=_cparams("parallel"), name="outproj_router",
    )(om, r, gd, x, g_mla, w_out, mod_l, g2, mod_l, mod_l, w_router_p, b_router_p)


def _expert_kernel(be_ref, nu_ref, x_ref, wgu_ref, bgu_ref, wdn_ref, bdn_ref, y_ref, wgu_s, wdn_s):
    i = pl.program_id(0)
    f = wdn_ref.shape[1]
    prev = be_ref[jnp.maximum(i - 1, 0)]

    @pl.when(jnp.logical_or(i == 0, be_ref[i] != prev))
    def _():
        wgu_s[...] = wgu_ref[0].astype(wgu_s.dtype)
        wdn_s[...] = wdn_ref[0].astype(wdn_s.dtype)

    @pl.when(i < nu_ref[0])
    def _():
        gu = _mm(x_ref[...], wgu_s[...]) + bgu_ref[0]
        gate = jnp.minimum(gu[:, :f], SWIGLU_LIMIT)
        up = jnp.clip(gu[:, f:], -SWIGLU_LIMIT, SWIGLU_LIMIT)
        act = (up + 1.0) * (gate * jax.nn.sigmoid(gate * SWIGLU_ALPHA))
        y_ref[...] = _mm(act, wdn_s[...]) + bdn_ref[0]

    @pl.when(i >= nu_ref[0])
    def _():
        y_ref[...] = jnp.zeros(y_ref.shape, F32)


def _experts(xs, block_e, n_used, w_gu, b_gu, w_dn, b_dn):
    rows, d = xs.shape
    e, _, f2 = w_gu.shape
    f = f2 // 2
    nblk = rows // MOE_BLOCK
    grid_spec = pltpu.PrefetchScalarGridSpec(
        num_scalar_prefetch=2,
        grid=(nblk,),
        in_specs=[pl.BlockSpec((MOE_BLOCK, d), lambda i, be, nu: (i, 0)),
                  pl.BlockSpec((1, d, f2), lambda i, be, nu: (be[i], 0, 0)),
                  pl.BlockSpec((1, 1, f2), lambda i, be, nu: (be[i], 0, 0)),
                  pl.BlockSpec((1, f, d), lambda i, be, nu: (be[i], 0, 0)),
                  pl.BlockSpec((1, 1, d), lambda i, be, nu: (be[i], 0, 0))],
        out_specs=pl.BlockSpec((MOE_BLOCK, d), lambda i, be, nu: (i, 0)),
        scratch_shapes=[pltpu.VMEM((d, f2), MXU_DTYPE), pltpu.VMEM((f, d), MXU_DTYPE)],
    )
    return pl.pallas_call(
        _expert_kernel,
        grid_spec=grid_spec,
        out_shape=jax.ShapeDtypeStruct((rows, d), F32),
        compiler_params=_cparams("arbitrary"), name="experts",
    )(block_e, n_used, xs, w_gu, b_gu.reshape(e, 1, f2), w_dn, b_dn.reshape(e, 1, d))


def _moe(h, top_i, top_w, w_gu, b_gu, w_dn, b_dn):
    t, d = h.shape
    e = w_gu.shape[0]
    idx = top_i[:, :TOP_K]
    gate = top_w[:, :TOP_K]
    member = jnp.sum(idx[:, :, None] == jnp.arange(e, dtype=jnp.int32)[None, None, :], 1).astype(jnp.int32)
    counts = jnp.sum(member, 0)
    rank = jnp.cumsum(member, 0) - member
    padded = (counts + MOE_BLOCK - 1) // MOE_BLOCK * MOE_BLOCK
    pend = jnp.cumsum(padded)
    pstart = pend - padded
    dest = pstart[idx] + jnp.take_along_axis(rank, idx, 1)
    nblk = -(-(t * TOP_K) // MOE_BLOCK) + e
    rows = nblk * MOE_BLOCK
    tok = jnp.broadcast_to(jnp.arange(t, dtype=jnp.int32)[:, None], (t, TOP_K))
    row_tok = jnp.full((rows,), t, jnp.int32).at[dest.reshape(-1)].set(tok.reshape(-1))
    xs = jnp.concatenate([h, jnp.zeros((1, d), h.dtype)], 0)[row_tok]
    block_e = jnp.minimum(jnp.searchsorted(pend, jnp.arange(nblk, dtype=jnp.int32) * MOE_BLOCK, side='right'),
                          e - 1).astype(jnp.int32)
    n_used = (pend[-1] // MOE_BLOCK).astype(jnp.int32).reshape(1)
    ys = _experts(xs, block_e, n_used, w_gu, b_gu, w_dn, b_dn)
    return jnp.sum(ys[dest] * gate[:, :, None], 1)


def _final_kernel(x_ref, y_ref, gt_ref, g_ref, o_ref):
    o_ref[...] = _rms(x_ref[...] + gt_ref[0] * y_ref[...], g_ref[...])


def _final(dims, x, y, mod_l, g_final):
    b, s, d, rb, l, ct = dims["B"], dims["S"], dims["D"], dims["RB"], dims["L"], dims["CT"]
    ns = s // rb
    src = lambda i: (i // ns) * (l // rb) + ct // rb + i % ns
    row = pl.BlockSpec((rb, d), lambda i: (src(i), 0))
    return pl.pallas_call(
        _final_kernel,
        grid=(b * ns,),
        in_specs=[row, row, pl.BlockSpec((1, 1, d), lambda i: (i // ns, 0, 5)),
                  pl.BlockSpec((1, d), lambda i: (0, 0))],
        out_specs=pl.BlockSpec((rb, d), lambda i: (i, 0)),
        out_shape=jax.ShapeDtypeStruct((b * s, d), F32),
        compiler_params=_cparams("parallel"), name="final_norm",
    )(x, y, mod_l, g_final)


def _rope_tables(s, rb, quarter, width):
    rows = (jnp.arange(s, dtype=jnp.int32) // GRID_W).astype(F32)
    cols = (jnp.arange(s, dtype=jnp.int32) % GRID_W).astype(F32)
    inv = ROPE_BASE ** (-jnp.arange(quarter, dtype=F32) / quarter)
    ar = rows[:, None] * inv[None, :]
    ac = cols[:, None] * inv[None, :]
    pad = width - 4 * quarter
    cos = jnp.concatenate([jnp.cos(ar), jnp.cos(ar), jnp.cos(ac), jnp.cos(ac), jnp.ones((s, pad), F32)], -1)
    sin = jnp.concatenate([-jnp.sin(ar), jnp.sin(ar), -jnp.sin(ac), jnp.sin(ac), jnp.zeros((s, pad), F32)], -1)
    cos = jnp.concatenate([cos, jnp.ones((rb, width), F32)], 0)
    sin = jnp.concatenate([sin, jnp.zeros((rb, width), F32)], 0)
    return cos, sin


def _layout_w_in(w):
    d = w.shape[0]
    n_mla = MLA_Q_LORA + MLA_KV_LORA + MLA_ROPE
    n_mid = 4 * RET_HEADS * RET_DK + 4 * GDN_HEADS * GDN_DK
    z = lambda n: jnp.zeros((d, n), w.dtype)
    out = jnp.concatenate([w[:, :n_mla], z(PB_RET_Q * LANE - n_mla), w[:, n_mla:n_mla + n_mid],
                           w[:, n_mla + n_mid:], z(LANE - 4 * GDN_HEADS)], -1)
    assert out.shape[1] == P_BLOCKS * LANE
    return out.astype(MXU_DTYPE)


def _layout_w_uq(w):
    r = w.shape[0]
    w = w.reshape(r, MLA_HEADS, MLA_NOPE + MLA_ROPE)
    w = jnp.concatenate([w, jnp.zeros((r, MLA_HEADS, MLA_QK - MLA_NOPE - MLA_ROPE), w.dtype)], -1)
    return w.reshape(r, MLA_HEADS * MLA_QK).astype(MXU_DTYPE)


def _layout_w_ukv(w):
    r = w.shape[0]
    w = w.reshape(r, MLA_HEADS, MLA_NOPE + MLA_V)
    wk = w[:, :, :MLA_NOPE].reshape(r, MLA_HEADS * MLA_NOPE)
    wv = w[:, :, MLA_NOPE:].reshape(r, MLA_HEADS * MLA_V)
    return wk.astype(MXU_DTYPE), wv.astype(MXU_DTYPE)


def kernel(x, c, ctx, c_ctx, w_mod, b_mod, g_norm1, g_norm2, w_in, g_cq, g_ckv, w_uq, w_ukv, g_mla, ret_decay, g_ret, gdn_conv, gdn_a_log, gdn_dt_bias, g_gdn, w_out, w_router, b_router, w_gate_up, b_gate_up, w_down, b_down, g_final):
    b, s, d = x.shape
    ct = ctx.shape[1]
    depth = w_mod.shape[0]
    n_experts = w_router.shape[-1]
    rb = math.gcd(ROW_BLOCK, ct)
    l = ct + s
    dims = dict(B=b, S=s, D=d, CT=ct, L=l, T=b * l, RB=rb)
    assert b + 1 <= 8 and ct % (2 * GDN_CHUNK) == 0 and s % rb == 0 and n_experts <= LANE

    stream = jnp.concatenate([ctx, x], 1).reshape(b * l, d)
    cs = jnp.concatenate([c, c_ctx[None, :], jnp.zeros((7 - b, d), F32)], 0)
    mod = _modulation(cs, w_mod, b_mod).reshape(depth, 8, 1, 6 * d)

    cos_m, sin_m = _rope_tables(s, rb, MLA_ROPE // 4, LANE)
    cos_r, sin_r = _rope_tables(s, 0, RET_DK // 4, LANE)
    y = None
    mod_prev = None
    for li in range(depth):
        w_in_p = _layout_w_in(w_in[li])
        stream, p = _inproj(dims, stream, y, mod[li], mod_prev, g_norm1[li][None, :], w_in_p)
        wk, wv = _layout_w_ukv(w_ukv[li])
        q_m, k_m, v_m = _mla_prep(dims, p, g_cq[li][None, :], g_ckv[li][None, :], _layout_w_uq(w_uq[li]),
                                  wk, wv, cos_m, sin_m)
        o_mla = _attention(dims, q_m, k_m, v_m)
        o_ret = _retention(dims, p, cos_r, sin_r, _retention_tables(ret_decay[li]), g_ret[li][None, :])
        conv_w = jnp.zeros((8, 3 * GDN_HEADS * LANE), F32).at[:GDN_CONV].set(gdn_conv[li])
        conv_w = conv_w.reshape(8, 3 * GDN_HEADS, LANE).transpose(1, 0, 2)
        gate_params = jnp.zeros((8, LANE), F32)
        gate_params = gate_params.at[0, :2 * GDN_HEADS].set(gdn_a_log[li].reshape(-1))
        gate_params = gate_params.at[1, :2 * GDN_HEADS].set(gdn_dt_bias[li].reshape(-1))
        o_gdn = _gdn(dims, p, conv_w, gate_params, g_gdn[li][None, :])
        w_router_p = jnp.zeros((d, LANE), F32).at[:, :n_experts].set(w_router[li]).astype(MXU_DTYPE)
        b_router_p = jnp.zeros((1, LANE), F32).at[0, :n_experts].set(b_router[li])
        stream, h2, top_i, top_w = _outproj(dims, o_mla, o_ret, o_gdn, stream, mod[li], g_mla[li][None, :],
                                            w_out[li].astype(MXU_DTYPE), g_norm2[li][None, :],
                                            w_router_p, b_router_p, n_experts)
        y = _moe(h2, top_i, top_w, w_gate_up[li], b_gate_up[li], w_down[li], b_down[li])
        mod_prev = mod[li]
    return _final(dims, stream, y, mod_prev, g_final[None, :]).reshape(b, s, d)
```

```python
import functools
import math

import jax
import jax.numpy as jnp
from jax import lax
from jax.experimental import pallas as pl
from jax.experimental.pallas import tpu as pltpu

F32 = jnp.float32
MXU_DTYPE = jnp.bfloat16

LANE = 128
V7X_VMEM_LIMIT_BYTES = 56 * 1024 * 1024

MLA_HEADS = 4
MLA_NOPE = 128
MLA_ROPE = 64
MLA_V = 128
MLA_Q_LORA = 256
MLA_KV_LORA = 128
MLA_QK = 256
RET_HEADS = 4
RET_DK = 128
GDN_HEADS = 4
GDN_DK = 128
GDN_CONV = 5
GDN_CHUNK = 64
GDN_BLOCK = 2 * GDN_CHUNK
GDN_INV_BASE = 16
RET_CHUNK = 128
TOP_K = 4
SWIGLU_LIMIT = 7.0
SWIGLU_ALPHA = 1.702
MOE_BLOCK = 256
GRID_W = 64
ROPE_BASE = 10000.0
NORM_EPS = 1e-6
GROUP_NORM_EPS = 1e-5
L2_EPS = 1e-6
ROW_BLOCK = 256

PB_DQ = 0
PB_DKV = 2
PB_RET_Q, PB_RET_K, PB_RET_V, PB_RET_G = 4, 8, 12, 16
PB_GDN_Q, PB_GDN_K, PB_GDN_V, PB_GDN_Z = 20, 24, 28, 32
PB_GDN_AB = 36
P_BLOCKS = 37


def _cparams(*sem):
    return pltpu.CompilerParams(dimension_semantics=sem, vmem_limit_bytes=V7X_VMEM_LIMIT_BYTES)


def _mm(a, b):
    return jnp.dot(a.astype(MXU_DTYPE), b.astype(MXU_DTYPE), preferred_element_type=F32)


def _mm_nt(a, b):
    return lax.dot_general(a.astype(MXU_DTYPE), b.astype(MXU_DTYPE), (((1,), (1,)), ((), ())),
                           preferred_element_type=F32)


def _mm_tn(a, b):
    return jnp.dot(a.T.astype(MXU_DTYPE), b.astype(MXU_DTYPE), preferred_element_type=F32)


def _rms(x, g, eps=NORM_EPS):
    return x * lax.rsqrt(jnp.mean(x * x, -1, keepdims=True) + eps) * g


def _silu(x):
    return x * jax.nn.sigmoid(x)


def _swap_pairs(x, q):
    n = x.shape[-1]
    lane = lax.broadcasted_iota(jnp.int32, x.shape, x.ndim - 1)
    first = (lane % (2 * q)) < q
    return jnp.where(first, pltpu.roll(x, n - q, x.ndim - 1), pltpu.roll(x, q, x.ndim - 1))


def _mod_kernel(c_ref, w_ref, b_ref, o_ref):
    o_ref[0] = _mm(_silu(c_ref[...]), w_ref[0]) + b_ref[0]


def _modulation(cs, w_mod, b_mod):
    depth, d, n = w_mod.shape
    tn = n // 6
    return pl.pallas_call(
        _mod_kernel,
        grid=(depth, n // tn),
        in_specs=[pl.BlockSpec((8, d), lambda l, j: (0, 0)),
                  pl.BlockSpec((1, d, tn), lambda l, j: (l, 0, j)),
                  pl.BlockSpec((1, 1, tn), lambda l, j: (l, 0, j))],
        out_specs=pl.BlockSpec((1, 8, tn), lambda l, j: (l, 0, j)),
        out_shape=jax.ShapeDtypeStruct((depth, 8, n), F32),
        compiler_params=_cparams("parallel", "parallel"),
        name="modulation",
    )(cs, w_mod, b_mod.reshape(depth, 1, n))


def _combine(tw_ref, y_refs):
    tw = tw_ref[...]
    y = y_refs[0][0] * tw[:, 0:1]
    for k in range(1, TOP_K):
        y = y + y_refs[k][0] * tw[:, k:k + 1]
    return y


def _inproj_kernel(has_y, *refs):
    if has_y:
        x_ref, tw_ref = refs[0], refs[1]
        y_refs = refs[2:2 + TOP_K]
        gt_ref, g_ref, sc_ref, sh_ref, w_ref, xo_ref, p_ref = refs[2 + TOP_K:]
        x = x_ref[...] + gt_ref[0] * _combine(tw_ref, y_refs)
        xo_ref[...] = x
    else:
        x_ref, g_ref, sc_ref, sh_ref, w_ref, p_ref = refs
        x = x_ref[...]
    h = _rms(x, g_ref[...]) * (1.0 + sc_ref[0]) + sh_ref[0]
    p_ref[...] = _mm(h, w_ref[...])


def _mod_spec(dims, chunk):
    d = dims["D"]
    return pl.BlockSpec((1, 1, d), lambda i: (_mod_index(dims, i), 0, chunk))


def _mod_index(dims, i):
    nb = dims["L"] // dims["RB"]
    return jnp.where(i % nb < dims["CT"] // dims["RB"], dims["B"], i // nb)


def _expert_row_specs(rb, d, row_block):
    return [pl.BlockSpec((1, rb, d), functools.partial(lambda k, i: (k, row_block(i), 0), k)) for k in range(TOP_K)]


def _inproj(dims, x, moe, mod_l, mod_prev, g1, w_in_p):
    t, d, rb = dims["T"], dims["D"], dims["RB"]
    np_ = w_in_p.shape[1]
    row = pl.BlockSpec((rb, d), lambda i: (i, 0))
    vec = pl.BlockSpec((1, d), lambda i: (0, 0))
    wspec = pl.BlockSpec((d, np_), lambda i: (0, 0))
    pspec = pl.BlockSpec((rb, np_), lambda i: (i, 0))
    pshape = jax.ShapeDtypeStruct((t, np_), F32)
    if moe is None:
        return x, pl.pallas_call(
            functools.partial(_inproj_kernel, False),
            grid=(t // rb,),
            in_specs=[row, vec, _mod_spec(dims, 1), _mod_spec(dims, 0), wspec],
            out_specs=pspec, out_shape=pshape,
            compiler_params=_cparams("parallel"), name="inproj",
        )(x, g1, mod_l, mod_l, w_in_p)
    ys, tw = moe
    return pl.pallas_call(
        functools.partial(_inproj_kernel, True),
        grid=(t // rb,),
        in_specs=[row, pl.BlockSpec((rb, LANE), lambda i: (i, 0))] + _expert_row_specs(rb, d, lambda i: i)
        + [_mod_spec(dims, 5), vec, _mod_spec(dims, 1), _mod_spec(dims, 0), wspec],
        out_specs=[row, pspec],
        out_shape=[jax.ShapeDtypeStruct((t, d), F32), pshape],
        compiler_params=_cparams("parallel"), name="inproj_res",
    )(x, tw, *([ys] * TOP_K), mod_prev, g1, mod_l, mod_l, w_in_p)


def _mla_prep_kernel(dq_ref, x2_ref, gq_ref, gkv_ref, wq_ref, wk_ref, wv_ref, cos_ref, sin_ref,
                     q_ref, k_ref, v_ref):
    cos = cos_ref[...]
    sin = sin_ref[...]

    def rope(r):
        return r * cos + _swap_pairs(r, MLA_ROPE // 4) * sin

    q = _mm(_rms(dq_ref[...], gq_ref[...]), wq_ref[...])
    x2 = x2_ref[...]
    kvn = _rms(x2[:, :MLA_KV_LORA], gkv_ref[...])
    kn = _mm(kvn, wk_ref[...])
    v_ref[...] = _mm(kvn, wv_ref[...]).astype(v_ref.dtype)
    kr = rope(x2[:, MLA_KV_LORA:]).astype(k_ref.dtype)
    for h in range(MLA_HEADS):
        o = h * MLA_QK
        q_ref[:, o:o + MLA_NOPE] = q[:, o:o + MLA_NOPE].astype(q_ref.dtype)
        q_ref[:, o + MLA_NOPE:o + MLA_QK] = rope(q[:, o + MLA_NOPE:o + MLA_QK]).astype(q_ref.dtype)
        k_ref[:, o:o + MLA_NOPE] = kn[:, h * MLA_NOPE:(h + 1) * MLA_NOPE].astype(k_ref.dtype)
        k_ref[:, o + MLA_NOPE:o + MLA_QK] = kr


def _pos_block(dims, i):
    nb = dims["L"] // dims["RB"]
    nc = dims["CT"] // dims["RB"]
    w = i % nb
    return jnp.where(w < nc, dims["S"] // dims["RB"], w - nc)


def _mla_prep(dims, p, g_cq, g_ckv, wq, wk, wv, cos_t, sin_t):
    t, rb = dims["T"], dims["RB"]
    hq = MLA_HEADS * MLA_QK
    hv = MLA_HEADS * MLA_V
    full = lambda a: pl.BlockSpec(a.shape, lambda i: (0, 0))
    tab = pl.BlockSpec((rb, LANE), lambda i: (_pos_block(dims, i), 0))
    return pl.pallas_call(
        _mla_prep_kernel,
        grid=(t // rb,),
        in_specs=[pl.BlockSpec((rb, 2 * LANE), lambda i: (i, 0)),
                  pl.BlockSpec((rb, 2 * LANE), lambda i: (i, 1)),
                  full(g_cq), full(g_ckv), full(wq), full(wk), full(wv), tab, tab],
        out_specs=[pl.BlockSpec((rb, hq), lambda i: (i, 0)),
                   pl.BlockSpec((rb, hq), lambda i: (i, 0)),
                   pl.BlockSpec((rb, hv), lambda i: (i, 0))],
        out_shape=[jax.ShapeDtypeStruct((t, hq), MXU_DTYPE),
                   jax.ShapeDtypeStruct((t, hq), MXU_DTYPE),
                   jax.ShapeDtypeStruct((t, hv), MXU_DTYPE)],
        compiler_params=_cparams("parallel"), name="mla_prep",
    )(p, p, g_cq, g_ckv, wq, wk, wv, cos_t, sin_t)


def _attn_kernel(n_ctx_blocks, ctx_len, q_ref, k_ref, v_ref, o_ref):
    scale = (MLA_NOPE + MLA_ROPE) ** -0.5
    half = q_ref.shape[0] // 2

    def attend(nk):
        ss = [_mm_nt(q_ref[r0:r0 + half, :], k_ref[0:nk, :]) * scale for r0 in (0, half)]
        es, dens = [], []
        for s in ss:
            e = jnp.exp(s - jnp.max(s, -1, keepdims=True))
            es.append(e)
            dens.append(jnp.sum(e, -1, keepdims=True))
        o_ref[...] = jnp.concatenate([_mm(e, v_ref[0:nk, :]) / den for e, den in zip(es, dens)], 0)

    i = pl.program_id(2)

    @pl.when(i < n_ctx_blocks)
    def _():
        attend(ctx_len)

    @pl.when(i >= n_ctx_blocks)
    def _():
        attend(k_ref.shape[0])


def _attention(dims, q, k, v):
    b, l, t, rb, ct = dims["B"], dims["L"], dims["T"], dims["RB"], dims["CT"]
    nq = l // rb
    return pl.pallas_call(
        functools.partial(_attn_kernel, ct // rb, ct),
        grid=(b, MLA_HEADS, nq),
        in_specs=[pl.BlockSpec((rb, MLA_QK), lambda bi, h, i: (bi * nq + i, h)),
                  pl.BlockSpec((l, MLA_QK), lambda bi, h, i: (bi, h)),
                  pl.BlockSpec((l, MLA_V), lambda bi, h, i: (bi, h))],
        out_specs=pl.BlockSpec((rb, MLA_V), lambda bi, h, i: (bi * nq + i, h)),
        out_shape=jax.ShapeDtypeStruct((t, MLA_HEADS * MLA_V), F32),
        compiler_params=_cparams("parallel", "parallel", "parallel"), name="mla_attention",
    )(q, k, v)


def _retention_kernel(qf_ref, kf_ref, vf_ref, cosf_ref, sinf_ref, qb_ref, kb_ref, vb_ref, cosb_ref, sinb_ref,
                      dm_ref, wk_ref, wq_ref, cd_ref, of_ref, ob_ref, s_ref):
    @pl.when(pl.program_id(1) == 0)
    def _():
        s_ref[...] = jnp.zeros(s_ref.shape, F32)

    scale = RET_DK ** -0.5
    chains = []
    for h in range(RET_HEADS):
        cols = slice(h * LANE, (h + 1) * LANE)
        for d, (q_ref, k_ref, v_ref, cos_ref, sin_ref, o_ref) in enumerate(
                ((qf_ref, kf_ref, vf_ref, cosf_ref, sinf_ref, of_ref),
                 (qb_ref, kb_ref, vb_ref, cosb_ref, sinb_ref, ob_ref))):
            rope = lambda x: x * cos_ref[...] + _swap_pairs(x, RET_DK // 4) * sin_ref[...]
            c = d * RET_HEADS + h
            chains.append(dict(q=rope(q_ref[:, cols]), k=rope(k_ref[:, cols]) * scale, v=v_ref[:, cols],
                               s=s_ref[c], c=c, o_ref=o_ref, cols=cols))
    for ch in chains:
        ch["sc"] = _mm_nt(ch["q"], ch["k"]) * dm_ref[ch["c"]]
    for ch in chains:
        ch["o"] = _mm(jnp.concatenate([ch["sc"], ch["q"] * wq_ref[ch["c"]]], -1),
                      jnp.concatenate([ch["v"], ch["s"]], 0))
    for ch in chains:
        c = ch["c"]
        ch["s"] = cd_ref[c][0:1, :] * ch["s"] + _mm_tn(ch["k"] * wk_ref[c], ch["v"])
    for ch in chains:
        ch["o_ref"][:, ch["cols"]] = ch["o"]
        s_ref[ch["c"]] = ch["s"]


def _retention_tables(ret_decay):
    c = RET_CHUNK
    lg = jnp.log1p(-jnp.exp(ret_decay.astype(F32))).reshape(2 * RET_HEADS)
    pos = jnp.arange(c, dtype=F32)
    diff = pos[:, None] - pos[None, :]
    lgm = lg[:, None, None]
    dm_f = jnp.where(diff >= 0, jnp.exp(lgm * jnp.maximum(diff, 0.0)), 0.0)
    dm_b = jnp.where(diff <= 0, jnp.exp(lgm * jnp.maximum(-diff, 0.0)), 0.0)
    dm = jnp.concatenate([dm_f[:RET_HEADS], dm_b[RET_HEADS:]], 0)
    bc = lambda a: jnp.broadcast_to(a[:, :, None], (2 * RET_HEADS, c, LANE))
    lg2 = lg[:, None]
    wk = bc(jnp.concatenate([jnp.exp(lg2 * (c - 1.0 - pos)[None, :])[:RET_HEADS],
                             jnp.exp(lg2 * pos[None, :])[RET_HEADS:]], 0))
    wq = bc(jnp.concatenate([jnp.exp(lg2 * (pos + 1.0)[None, :])[:RET_HEADS],
                             jnp.exp(lg2 * (c - pos)[None, :])[RET_HEADS:]], 0))
    cd = jnp.broadcast_to(jnp.exp(lg * c)[:, None, None], (2 * RET_HEADS, 8, LANE))
    return dm, wk, wq, cd


def _retention(dims, p, cos_t, sin_t, tables):
    b, l, t, s, ct = dims["B"], dims["L"], dims["T"], dims["S"], dims["CT"]
    c = RET_CHUNK
    n = l // c
    n_ctx = ct // c
    hw = RET_HEADS * LANE
    dm, wk, wq, cd = tables
    fwd = lambda i: i
    bwd = lambda i: jnp.where(i < n_ctx, n_ctx - 1 - i, n - 1 + n_ctx - i)
    tab = lambda blkf: pl.BlockSpec((c, LANE), lambda bi, i: (jnp.where(blkf(i) < n_ctx, s // c, blkf(i) - n_ctx), 0))
    qkv = lambda blkf: [pl.BlockSpec((c, hw), functools.partial(lambda j, bi, i: (bi * n + blkf(i), j), j))
                        for j in (PB_RET_Q * LANE // hw, PB_RET_K * LANE // hw, PB_RET_V * LANE // hw)]
    full = lambda a: pl.BlockSpec(a.shape, lambda bi, i: (0, 0, 0))
    return pl.pallas_call(
        _retention_kernel,
        grid=(b, n),
        in_specs=qkv(fwd) + [tab(fwd), tab(fwd)] + qkv(bwd) + [tab(bwd), tab(bwd)]
        + [full(dm), full(wk), full(wq), full(cd)],
        out_specs=[pl.BlockSpec((c, hw), lambda bi, i: (bi * n + fwd(i), 0)),
                   pl.BlockSpec((c, hw), lambda bi, i: (bi * n + bwd(i), 0))],
        out_shape=[jax.ShapeDtypeStruct((t, hw), F32), jax.ShapeDtypeStruct((t, hw), F32)],
        scratch_shapes=[pltpu.VMEM((2 * RET_HEADS, RET_DK, LANE), F32)],
        compiler_params=_cparams("parallel", "arbitrary"), name="retention",
    )(p, p, p, cos_t, sin_t, p, p, p, cos_t, sin_t, dm, wk, wq, cd)


def _gdn_conv_kernel(ctx_len, tile, x_ref, w_ref, o_ref):
    l = x_ref.shape[0]
    cb = pl.program_id(1)
    w = w_ref[0]
    half = GDN_CONV // 2
    row = lax.broadcasted_iota(jnp.int32, (tile, LANE), 0)
    for r0 in range(0, l, tile):
        seg0, seg1 = (0, ctx_len) if r0 < ctx_len else (ctx_len, l)
        x = x_ref[r0:r0 + tile, :]
        acc = x * w[half:half + 1, :]
        for d in range(1, half + 1):
            if r0 - d >= seg0:
                back = x_ref[r0 - d:r0 - d + tile, :]
            else:
                back = jnp.where(row >= d, pltpu.roll(x, d, 0), 0.0)
            if r0 + tile + d <= seg1:
                fwd = x_ref[r0 + d:r0 + d + tile, :]
            else:
                fwd = jnp.where(row < tile - d, pltpu.roll(x, tile - d, 0), 0.0)
            acc = acc + back * w[half - d:half - d + 1, :] + fwd * w[half + d:half + d + 1, :]
        y = _silu(acc)
        nrm = lax.rsqrt(jnp.sum(y * y, -1, keepdims=True) + L2_EPS)
        fac = jnp.where(cb < 2 * GDN_HEADS, nrm, 1.0) * jnp.where(cb < GDN_HEADS, GDN_DK ** -0.5, 1.0)
        o_ref[r0:r0 + tile, :] = y * fac


def _gdn_conv(dims, p, conv_w):
    b, l, t, ct, rb = dims["B"], dims["L"], dims["T"], dims["CT"], dims["RB"]
    ncb = 3 * GDN_HEADS
    return pl.pallas_call(
        functools.partial(_gdn_conv_kernel, ct, rb),
        grid=(b, ncb),
        in_specs=[pl.BlockSpec((l, LANE), lambda bi, c: (bi, PB_GDN_Q + c)),
                  pl.BlockSpec((1, 8, LANE), lambda bi, c: (c, 0, 0))],
        out_specs=pl.BlockSpec((l, LANE), lambda bi, c: (bi, c)),
        out_shape=jax.ShapeDtypeStruct((t, ncb * LANE), F32),
        compiler_params=_cparams("parallel", "parallel"), name="gdn_conv",
    )(p, conv_w)


def _gdn_chain_kernel(qf_ref, kf_ref, vf_ref, abf_ref, qb_ref, kb_ref, vb_ref, abb_ref, gp_ref,
                      of_ref, ob_ref, s_ref):
    blk = GDN_BLOCK

    @pl.when(pl.program_id(1) == 0)
    def _():
        s_ref[...] = jnp.zeros(s_ref.shape, F32)

    ri = lax.broadcasted_iota(jnp.int32, (blk, blk), 0)
    ci = lax.broadcasted_iota(jnp.int32, (blk, blk), 1)
    same = (ri // GDN_CHUNK) == (ci // GDN_CHUNK)
    sub = [(ri // m) == (ci // m) for m in (GDN_INV_BASE << i for i in range(8)) if m <= GDN_CHUNK]
    eye = jnp.where(ri == ci, 1.0, 0.0)
    lane = lax.broadcasted_iota(jnp.int32, (blk, LANE), 1)
    row = lax.broadcasted_iota(jnp.int32, (blk, LANE), 0)
    row_lo = row < GDN_CHUNK
    pos = row % GDN_CHUNK
    gp = gp_ref[...]

    def gates(ab):
        sp = ab + gp[1:2, :]
        la = -jnp.exp(gp[0:1, :]) * (jnp.maximum(sp, 0.0) + jnp.log1p(jnp.exp(-jnp.abs(sp))))
        return la, jax.nn.sigmoid(ab)

    la_f, beta_f = gates(abf_ref[...])
    la_b, beta_b = gates(abb_ref[...])
    g_f = la_f
    g_b = la_b
    sh = 1
    while sh < GDN_CHUNK:
        g_f = g_f + jnp.where(pos >= sh, pltpu.roll(g_f, sh, 0), 0.0)
        g_b = g_b + jnp.where(pos < GDN_CHUNK - sh, pltpu.roll(g_b, blk - sh, 0), 0.0)
        sh *= 2

    def pick(x, c):
        return jnp.broadcast_to(jnp.sum(jnp.where(lane == c, x, 0.0), -1, keepdims=True), (blk, LANE))

    chains = []
    for h in range(GDN_HEADS):
        cols = slice(h * LANE, (h + 1) * LANE)
        for d, (q_ref, k_ref, v_ref, g, beta, o_ref) in enumerate(
                ((qf_ref, kf_ref, vf_ref, g_f, beta_f, of_ref), (qb_ref, kb_ref, vb_ref, g_b, beta_b, ob_ref))):
            c = d * GDN_HEADS + h
            chains.append(dict(q=q_ref[:, cols], k=k_ref[:, cols], v=v_ref[:, cols], gb=pick(g, c),
                               beta=pick(beta, 2 * GDN_HEADS + c), s=s_ref[c], bwd=d == 1,
                               o_ref=o_ref, cols=cols, c=c, outs=[None, None]))

    for ch in chains:
        gb = ch["gb"]
        gdiff = gb - gb.T
        incl = same & ((ci >= ri) if ch["bwd"] else (ci <= ri))
        ch["strict"] = same & ((ci > ri) if ch["bwd"] else (ci < ri))
        ch["decay"] = jnp.where(incl, jnp.exp(jnp.where(incl, gdiff, 0.0)), 0.0)
        ch["kb"] = ch["k"] * ch["beta"]
    for ch in chains:
        sc = _mm_nt(jnp.concatenate([ch["kb"], ch["q"]], 0), ch["k"])
        ch["a"] = jnp.where(ch["strict"], sc[:blk, :] * ch["decay"], 0.0)
        ch["attn"] = sc[blk:, :] * ch["decay"]
        ch["p"] = -jnp.where(sub[0], ch["a"], 0.0)
        ch["t"] = eye + ch["p"]
    for ch in chains:
        ch["p"] = _mm(ch["p"], ch["p"])
    for _ in range(int(math.log2(GDN_INV_BASE)) - 2):
        for ch in chains:
            r = _mm(jnp.concatenate([ch["p"], ch["t"]], 0), ch["p"])
            ch["t"] = ch["t"] + r[blk:, :]
            ch["p"] = r[:blk, :]
    for ch in chains:
        ch["t"] = ch["t"] + _mm(ch["t"], ch["p"])
    for lvl in range(1, len(sub)):
        for ch in chains:
            a_off = jnp.where(sub[lvl] & jnp.logical_not(sub[lvl - 1]), ch["a"], 0.0)
            ch["x"] = _mm(a_off, ch["t"])
        for ch in chains:
            ch["t"] = ch["t"] - _mm(ch["t"], ch["x"])
    for ch in chains:
        eg = jnp.exp(ch["gb"])
        uw = _mm(ch["t"], jnp.concatenate([ch["v"] * ch["beta"], ch["kb"] * eg], -1))
        ch["u"], ch["w"] = uw[:, :LANE], uw[:, LANE:]
        ch["q_in"] = ch["q"] * eg
        ch["ends"] = (0, GDN_CHUNK) if ch["bwd"] else (GDN_CHUNK - 1, blk - 1)
        e0, e1 = ch["ends"]
        g_end = jnp.where(row_lo, ch["gb"][e0:e0 + 1, :], ch["gb"][e1:e1 + 1, :])
        ch["k_tail"] = ch["k"] * jnp.exp(g_end - ch["gb"])
    for step in range(2):
        for ch in chains:
            c = (1 - step) if ch["bwd"] else step
            rc = slice(c * GDN_CHUNK, (c + 1) * GDN_CHUNK)
            v_new = ch["u"][rc, :] - _mm(ch["w"][rc, :], ch["s"])
            zero = jnp.zeros_like(v_new)
            ch["vn"] = jnp.concatenate([v_new, zero] if c == 0 else [zero, v_new], 0)
        for ch in chains:
            c = (1 - step) if ch["bwd"] else step
            rc = slice(c * GDN_CHUNK, (c + 1) * GDN_CHUNK)
            ch["outs"][c] = _mm(jnp.concatenate([ch["attn"][rc, :], ch["q_in"][rc, :]], -1),
                                jnp.concatenate([ch["vn"], ch["s"]], 0))
        for ch in chains:
            c = (1 - step) if ch["bwd"] else step
            in_c = row_lo if c == 0 else jnp.logical_not(row_lo)
            e = ch["ends"][c]
            ch["s"] = ch["s"] * jnp.exp(ch["gb"][e:e + 1, :]) \
                + _mm_tn(jnp.where(in_c, ch["k_tail"], 0.0), ch["vn"])
    for ch in chains:
        ch["o_ref"][:, ch["cols"]] = jnp.concatenate(ch["outs"], 0)
        s_ref[ch["c"]] = ch["s"]


def _gdn_chain(dims, p, qkv, gate_params):
    b, l, t, ct = dims["B"], dims["L"], dims["T"], dims["CT"]
    blk = GDN_BLOCK
    n = l // blk
    n_ctx = ct // blk
    hw = GDN_HEADS * LANE
    fwd = lambda bi, i: bi * n + i
    bwd = lambda bi, i: bi * n + jnp.where(i < n_ctx, n_ctx - 1 - i, n - 1 + n_ctx - i)
    qkv_specs = lambda rowf: [pl.BlockSpec((blk, hw), functools.partial(lambda j, bi, i: (rowf(bi, i), j), j))
                              for j in range(3)]
    ab_spec = lambda rowf: pl.BlockSpec((blk, LANE), lambda bi, i: (rowf(bi, i), PB_GDN_AB))
    return pl.pallas_call(
        _gdn_chain_kernel,
        grid=(b, n),
        in_specs=qkv_specs(fwd) + [ab_spec(fwd)] + qkv_specs(bwd) + [ab_spec(bwd)]
        + [pl.BlockSpec((8, LANE), lambda bi, i: (0, 0))],
        out_specs=[pl.BlockSpec((blk, hw), lambda bi, i: (fwd(bi, i), 0)),
                   pl.BlockSpec((blk, hw), lambda bi, i: (bwd(bi, i), 0))],
        out_shape=[jax.ShapeDtypeStruct((t, hw), F32), jax.ShapeDtypeStruct((t, hw), F32)],
        scratch_shapes=[pltpu.VMEM((2 * GDN_HEADS, GDN_DK, LANE), F32)],
        compiler_params=_cparams("parallel", "arbitrary"), name="gdn_chain",
    )(qkv, qkv, qkv, p, qkv, qkv, qkv, p, gate_params)


def _outproj_kernel(n_experts, om_ref, rf_ref, rb_ref, rg_ref, gf_ref, gb_ref, z_ref, x_ref, gm_ref, gr_ref,
                    gg_ref, w_ref, gt_ref, g2_ref, sc_ref, sh_ref, wr_ref, br_ref, xo_ref, hp_ref, ti_ref, tw_ref):
    ret, gdn = [], []
    for h in range(RET_HEADS):
        cols = slice(h * LANE, (h + 1) * LANE)
        o = rf_ref[:, cols] + rb_ref[:, cols]
        oc = o - jnp.mean(o, -1, keepdims=True)
        var = jnp.mean(oc * oc, -1, keepdims=True)
        ret.append(oc * lax.rsqrt(var + GROUP_NORM_EPS) * gr_ref[:, cols] * _silu(rg_ref[:, cols]))
    for h in range(GDN_HEADS):
        cols = slice(h * LANE, (h + 1) * LANE)
        o = gf_ref[:, cols] + gb_ref[:, cols]
        gdn.append(_rms(o, gg_ref[...]) * _silu(z_ref[:, cols]))
    mixed = jnp.concatenate([_rms(om_ref[...], gm_ref[...])] + ret + gdn, -1)
    x = x_ref[...] + gt_ref[0] * _mm(mixed, w_ref[...])
    xo_ref[...] = x
    h = (_rms(x, g2_ref[...]) * (1.0 + sc_ref[0]) + sh_ref[0]).astype(MXU_DTYPE)
    hp_ref[...] = h.astype(F32)
    logits = _mm(h, wr_ref[...]) + br_ref[...]
    lane = lax.broadcasted_iota(jnp.int32, logits.shape, 1)
    lane_f = lane.astype(F32)
    neg = jnp.float32(-jnp.inf)
    logits = jnp.where(lane < n_experts, logits, neg)
    vals, idxs = [], []
    for _ in range(TOP_K):
        m = jnp.max(logits, -1, keepdims=True)
        idx = jnp.min(jnp.where(logits == m, lane_f, float(LANE)), -1, keepdims=True)
        logits = jnp.where(lane_f == idx, neg, logits)
        vals.append(m)
        idxs.append(idx.astype(jnp.int32))
    es = [jnp.exp(v - vals[0]) for v in vals]
    den = es[0]
    for e in es[1:]:
        den = den + e
    ti = jnp.zeros(logits.shape, jnp.int32)
    tw = jnp.zeros(logits.shape, F32)
    for k in range(TOP_K):
        ti = jnp.where(lane == k, idxs[k], ti)
        tw = jnp.where(lane == k, es[k] / den, tw)
    ti_ref[...] = ti
    tw_ref[...] = tw


def _outproj(dims, om, rf, rbk, gf, gb, p, x, mod_l, g_mla, g_ret, g_gdn, w_out, g2, w_router_p, b_router_p,
             n_experts):
    t, d, rb = dims["T"], dims["D"], dims["RB"]
    hw = GDN_HEADS * LANE
    row = lambda w: pl.BlockSpec((rb, w), lambda i: (i, 0))
    full = lambda a: pl.BlockSpec(a.shape, lambda i: (0, 0))
    pcol = lambda blk: pl.BlockSpec((rb, hw), lambda i: (i, blk * LANE // hw))
    return pl.pallas_call(
        functools.partial(_outproj_kernel, n_experts),
        grid=(t // rb,),
        in_specs=[row(om.shape[1]), row(hw), row(hw), pcol(PB_RET_G), row(hw), row(hw), pcol(PB_GDN_Z), row(d),
                  full(g_mla), full(g_ret), full(g_gdn), full(w_out),
                  _mod_spec(dims, 2), full(g2), _mod_spec(dims, 4), _mod_spec(dims, 3),
                  full(w_router_p), full(b_router_p)],
        out_specs=[row(d), row(d), row(LANE), row(LANE)],
        out_shape=[jax.ShapeDtypeStruct((t, d), F32), jax.ShapeDtypeStruct((t, d), F32),
                   jax.ShapeDtypeStruct((t, LANE), jnp.int32), jax.ShapeDtypeStruct((t, LANE), F32)],
        compiler_params=_cparams("parallel"), name="outproj_router",
    )(om, rf, rbk, p, gf, gb, p, x, g_mla, g_ret, g_gdn, w_out, mod_l, g2, mod_l, mod_l, w_router_p, b_router_p)


def _expert_kernel(be_ref, nu_ref, x_ref, wgu_ref, bgu_ref, wdn_ref, bdn_ref, y_ref, wgu_s, wdn_s):
    i = pl.program_id(0)
    f = wdn_ref.shape[2]
    prev = be_ref[jnp.maximum(i - 1, 0)]

    @pl.when(jnp.logical_or(i == 0, be_ref[i] != prev))
    def _():
        wgu_s[...] = wgu_ref[0, 0].astype(wgu_s.dtype)
        wdn_s[...] = wdn_ref[0, 0].astype(wdn_s.dtype)

    @pl.when(i < nu_ref[0])
    def _():
        gu = _mm(x_ref[...], wgu_s[...]) + bgu_ref[0]
        gate = jnp.minimum(gu[:, :f], SWIGLU_LIMIT)
        up = jnp.clip(gu[:, f:], -SWIGLU_LIMIT, SWIGLU_LIMIT)
        act = (up + 1.0) * (gate * jax.nn.sigmoid(gate * SWIGLU_ALPHA))
        y_ref[...] = _mm(act, wdn_s[...]) + bdn_ref[0]

    @pl.when(i >= nu_ref[0])
    def _():
        y_ref[...] = jnp.zeros(y_ref.shape, F32)


def _experts(li, xs, block_e, n_used, w_gu, b_gu, w_dn, b_dn):
    rows = xs.shape[0]
    depth, e, d, f2 = w_gu.shape
    f = f2 // 2
    nblk = rows // MOE_BLOCK
    grid_spec = pltpu.PrefetchScalarGridSpec(
        num_scalar_prefetch=2,
        grid=(nblk,),
        in_specs=[pl.BlockSpec((MOE_BLOCK, d), lambda i, be, nu: (i, 0)),
                  pl.BlockSpec((1, 1, d, f2), lambda i, be, nu: (li, be[i], 0, 0)),
                  pl.BlockSpec((1, 1, f2), lambda i, be, nu: (li * e + be[i], 0, 0)),
                  pl.BlockSpec((1, 1, f, d), lambda i, be, nu: (li, be[i], 0, 0)),
                  pl.BlockSpec((1, 1, d), lambda i, be, nu: (li * e + be[i], 0, 0))],
        out_specs=pl.BlockSpec((MOE_BLOCK, d), lambda i, be, nu: (i, 0)),
        scratch_shapes=[pltpu.VMEM((d, f2), MXU_DTYPE), pltpu.VMEM((f, d), MXU_DTYPE)],
    )
    return pl.pallas_call(
        _expert_kernel,
        grid_spec=grid_spec,
        out_shape=jax.ShapeDtypeStruct((rows, d), F32),
        compiler_params=_cparams("arbitrary"), name="experts",
    )(block_e, n_used, xs, w_gu, b_gu.reshape(depth * e, 1, f2), w_dn, b_dn.reshape(depth * e, 1, d))


def _moe(li, hp, top_i, w_gu, b_gu, w_dn, b_dn):
    t = hp.shape[0]
    e = w_gu.shape[1]
    idx = top_i[:, :TOP_K]
    onehot = idx[:, :, None] == jnp.arange(e, dtype=jnp.int32)[None, None, :]
    member = jnp.sum(onehot, 1).astype(jnp.int32)
    counts = jnp.sum(member, 0)
    rank = jnp.cumsum(member, 0) - member
    padded = (counts + MOE_BLOCK - 1) // MOE_BLOCK * MOE_BLOCK
    pend = jnp.cumsum(padded)
    pstart = pend - padded
    dest = jnp.sum(jnp.where(onehot, (rank + pstart[None, :])[:, None, :], 0), -1)
    nblk = -(-(t * TOP_K) // MOE_BLOCK) + e
    rows = nblk * MOE_BLOCK
    tok = jnp.broadcast_to(jnp.arange(t, dtype=jnp.int32)[:, None], (t, TOP_K))
    row_tok = jnp.zeros((rows,), jnp.int32).at[dest.reshape(-1)].set(tok.reshape(-1))
    xs = hp[row_tok]
    blk_row = jnp.arange(nblk, dtype=jnp.int32) * MOE_BLOCK
    block_e = jnp.minimum(jnp.sum(pend[None, :] <= blk_row[:, None], 1), e - 1).astype(jnp.int32)
    n_used = (pend[-1] // MOE_BLOCK).astype(jnp.int32).reshape(1)
    ys = _experts(li, xs, block_e, n_used, w_gu, b_gu, w_dn, b_dn)
    return ys[dest.T.reshape(-1)].reshape(TOP_K, t, ys.shape[1])


def _final_kernel(x_ref, tw_ref, *refs):
    y_refs = refs[:TOP_K]
    gt_ref, g_ref, o_ref = refs[TOP_K:]
    o_ref[...] = _rms(x_ref[...] + gt_ref[0] * _combine(tw_ref, y_refs), g_ref[...])


def _final(dims, x, moe, mod_l, g_final):
    b, s, d, rb, l, ct = dims["B"], dims["S"], dims["D"], dims["RB"], dims["L"], dims["CT"]
    ys, tw = moe
    ns = s // rb
    src = lambda i: (i // ns) * (l // rb) + ct // rb + i % ns
    return pl.pallas_call(
        _final_kernel,
        grid=(b * ns,),
        in_specs=[pl.BlockSpec((rb, d), lambda i: (src(i), 0)), pl.BlockSpec((rb, LANE), lambda i: (src(i), 0))]
        + _expert_row_specs(rb, d, src)
        + [pl.BlockSpec((1, 1, d), lambda i: (i // ns, 0, 5)), pl.BlockSpec((1, d), lambda i: (0, 0))],
        out_specs=pl.BlockSpec((rb, d), lambda i: (i, 0)),
        out_shape=jax.ShapeDtypeStruct((b * s, d), F32),
        compiler_params=_cparams("parallel"), name="final_norm",
    )(x, tw, *([ys] * TOP_K), mod_l, g_final)


def _rope_tables(s, rb, quarter, width):
    rows = (jnp.arange(s, dtype=jnp.int32) // GRID_W).astype(F32)
    cols = (jnp.arange(s, dtype=jnp.int32) % GRID_W).astype(F32)
    inv = ROPE_BASE ** (-jnp.arange(quarter, dtype=F32) / quarter)
    ar = rows[:, None] * inv[None, :]
    ac = cols[:, None] * inv[None, :]
    pad = width - 4 * quarter
    cos = jnp.concatenate([jnp.cos(ar), jnp.cos(ar), jnp.cos(ac), jnp.cos(ac), jnp.ones((s, pad), F32)], -1)
    sin = jnp.concatenate([-jnp.sin(ar), jnp.sin(ar), -jnp.sin(ac), jnp.sin(ac), jnp.zeros((s, pad), F32)], -1)
    cos = jnp.concatenate([cos, jnp.ones((rb, width), F32)], 0)
    sin = jnp.concatenate([sin, jnp.zeros((rb, width), F32)], 0)
    return cos, sin


def _layout_w_in(w):
    d = w.shape[0]
    n_mla = MLA_Q_LORA + MLA_KV_LORA + MLA_ROPE
    n_mid = 4 * RET_HEADS * RET_DK + 4 * GDN_HEADS * GDN_DK
    z = lambda n: jnp.zeros((d, n), w.dtype)
    out = jnp.concatenate([w[:, :n_mla], z(PB_RET_Q * LANE - n_mla), w[:, n_mla:n_mla + n_mid],
                           w[:, n_mla + n_mid:], z(LANE - 4 * GDN_HEADS)], -1)
    assert out.shape[1] == P_BLOCKS * LANE
    return out.astype(MXU_DTYPE)


def _layout_w_uq(w):
    r = w.shape[0]
    w = w.reshape(r, MLA_HEADS, MLA_NOPE + MLA_ROPE)
    w = jnp.concatenate([w, jnp.zeros((r, MLA_HEADS, MLA_QK - MLA_NOPE - MLA_ROPE), w.dtype)], -1)
    return w.reshape(r, MLA_HEADS * MLA_QK).astype(MXU_DTYPE)


def _layout_w_ukv(w):
    r = w.shape[0]
    w = w.reshape(r, MLA_HEADS, MLA_NOPE + MLA_V)
    wk = w[:, :, :MLA_NOPE].reshape(r, MLA_HEADS * MLA_NOPE)
    wv = w[:, :, MLA_NOPE:].reshape(r, MLA_HEADS * MLA_V)
    return wk.astype(MXU_DTYPE), wv.astype(MXU_DTYPE)


def kernel(x, c, ctx, c_ctx, w_mod, b_mod, g_norm1, g_norm2, w_in, g_cq, g_ckv, w_uq, w_ukv, g_mla, ret_decay, g_ret, gdn_conv, gdn_a_log, gdn_dt_bias, g_gdn, w_out, w_router, b_router, w_gate_up, b_gate_up, w_down, b_down, g_final):
    b, s, d = x.shape
    ct = ctx.shape[1]
    depth = w_mod.shape[0]
    n_experts = w_router.shape[-1]
    rb = math.gcd(ROW_BLOCK, ct)
    l = ct + s
    dims = dict(B=b, S=s, D=d, CT=ct, L=l, T=b * l, RB=rb)
    assert b + 1 <= 8 and ct % GDN_BLOCK == 0 and s % rb == 0 and n_experts <= LANE

    stream = jnp.concatenate([ctx, x], 1).reshape(b * l, d)
    cs = jnp.concatenate([c, c_ctx[None, :], jnp.zeros((7 - b, d), F32)], 0)
    mod = _modulation(cs, w_mod, b_mod).reshape(depth, 8, 1, 6 * d)

    cos_m, sin_m = _rope_tables(s, rb, MLA_ROPE // 4, LANE)
    cos_r, sin_r = _rope_tables(s, RET_CHUNK, RET_DK // 4, LANE)
    moe = None
    mod_prev = None
    for li in range(depth):
        w_in_p = _layout_w_in(w_in[li])
        stream, p = _inproj(dims, stream, moe, mod[li], mod_prev, g_norm1[li][None, :], w_in_p)
        wk, wv = _layout_w_ukv(w_ukv[li])
        q_m, k_m, v_m = _mla_prep(dims, p, g_cq[li][None, :], g_ckv[li][None, :], _layout_w_uq(w_uq[li]),
                                  wk, wv, cos_m, sin_m)
        o_mla = _attention(dims, q_m, k_m, v_m)
        o_rf, o_rb = _retention(dims, p, cos_r, sin_r, _retention_tables(ret_decay[li]))
        conv_w = jnp.zeros((8, 3 * GDN_HEADS * LANE), F32).at[:GDN_CONV].set(gdn_conv[li])
        conv_w = conv_w.reshape(8, 3 * GDN_HEADS, LANE).transpose(1, 0, 2)
        gate_params = jnp.zeros((8, LANE), F32)
        gate_params = gate_params.at[0, :2 * GDN_HEADS].set(gdn_a_log[li].reshape(-1))
        gate_params = gate_params.at[1, :2 * GDN_HEADS].set(gdn_dt_bias[li].reshape(-1))
        o_gf, o_gb = _gdn_chain(dims, p, _gdn_conv(dims, p, conv_w), gate_params)
        w_router_p = jnp.zeros((d, LANE), F32).at[:, :n_experts].set(w_router[li]).astype(MXU_DTYPE)
        b_router_p = jnp.zeros((1, LANE), F32).at[0, :n_experts].set(b_router[li])
        stream, hp, top_i, top_w = _outproj(dims, o_mla, o_rf, o_rb, o_gf, o_gb, p, stream, mod[li],
                                            g_mla[li][None, :], g_ret[li][None, :], g_gdn[li][None, :],
                                            w_out[li].astype(MXU_DTYPE),
                                            g_norm2[li][None, :], w_router_p, b_router_p, n_experts)
        moe = (_moe(li, hp, top_i, w_gate_up, b_gate_up, w_down, b_down), top_w)
        mod_prev = mod[li]
    return _final(dims, stream, moe, mod_prev, g_final[None, :]).reshape(b, s, d)
```

```python
import functools
import math

import jax
import jax.numpy as jnp
from jax import lax
from jax.experimental import pallas as pl
from jax.experimental.pallas import tpu as pltpu

F32 = jnp.float32
MXU_DTYPE = jnp.bfloat16

LANE = 128
V7X_VMEM_LIMIT_BYTES = 56 * 1024 * 1024

MLA_HEADS = 4
MLA_NOPE = 128
MLA_ROPE = 64
MLA_V = 128
MLA_Q_LORA = 256
MLA_KV_LORA = 128
MLA_QK = 256
MLA_VW = 256
RET_HEADS = 4
RET_DK = 128
GDN_HEADS = 4
GDN_DK = 128
GDN_CONV = 5
GDN_CHUNK = 64
GDN_BLOCK = 2 * GDN_CHUNK
GDN_INV_BASE = 16
RET_CHUNK = 128
TOP_K = 4
SWIGLU_LIMIT = 7.0
SWIGLU_ALPHA = 1.702
MOE_BLOCK = 256
GRID_W = 64
ROPE_BASE = 10000.0
NORM_EPS = 1e-6
GROUP_NORM_EPS = 1e-5
L2_EPS = 1e-6
ROW_BLOCK = 256

PB_DQ = 0
PB_DKV = 2
PB_RET_Q, PB_RET_K, PB_RET_V, PB_RET_G = 4, 8, 12, 16
PB_GDN_Q, PB_GDN_K, PB_GDN_V, PB_GDN_Z = 20, 24, 28, 32
PB_GDN_AB = 36
P_BLOCKS = 37


def _cparams(*sem):
    return pltpu.CompilerParams(dimension_semantics=sem, vmem_limit_bytes=V7X_VMEM_LIMIT_BYTES)


def _mm(a, b):
    return jnp.dot(a.astype(MXU_DTYPE), b.astype(MXU_DTYPE), preferred_element_type=F32)


def _mm_nt(a, b):
    return lax.dot_general(a.astype(MXU_DTYPE), b.astype(MXU_DTYPE), (((1,), (1,)), ((), ())),
                           preferred_element_type=F32)


def _mm_tn(a, b):
    return jnp.dot(a.T.astype(MXU_DTYPE), b.astype(MXU_DTYPE), preferred_element_type=F32)


def _rms(x, g, eps=NORM_EPS):
    return x * lax.rsqrt(jnp.mean(x * x, -1, keepdims=True) + eps) * g


def _silu(x):
    return x * jax.nn.sigmoid(x)


def _swap_pairs(x, q):
    n = x.shape[-1]
    lane = lax.broadcasted_iota(jnp.int32, x.shape, x.ndim - 1)
    first = (lane % (2 * q)) < q
    return jnp.where(first, pltpu.roll(x, n - q, x.ndim - 1), pltpu.roll(x, q, x.ndim - 1))


def _mod_kernel(c_ref, w_ref, b_ref, o_ref):
    o_ref[0] = _mm(_silu(c_ref[...]), w_ref[0]) + b_ref[0]


def _modulation(cs, w_mod, b_mod):
    depth, d, n = w_mod.shape
    tn = n // 6
    return pl.pallas_call(
        _mod_kernel,
        grid=(depth, n // tn),
        in_specs=[pl.BlockSpec((8, d), lambda l, j: (0, 0)),
                  pl.BlockSpec((1, d, tn), lambda l, j: (l, 0, j)),
                  pl.BlockSpec((1, 1, tn), lambda l, j: (l, 0, j))],
        out_specs=pl.BlockSpec((1, 8, tn), lambda l, j: (l, 0, j)),
        out_shape=jax.ShapeDtypeStruct((depth, 8, n), F32),
        compiler_params=_cparams("parallel", "parallel"),
        name="modulation",
    )(cs, w_mod, b_mod.reshape(depth, 1, n))


def _combine(tw_ref, y_refs):
    tw = tw_ref[...]
    y = y_refs[0][0] * tw[:, 0:1]
    for k in range(1, TOP_K):
        y = y + y_refs[k][0] * tw[:, k:k + 1]
    return y


def _inproj_kernel(has_y, *refs):
    if has_y:
        x_ref, tw_ref = refs[0], refs[1]
        y_refs = refs[2:2 + TOP_K]
        gt_ref, g_ref, sc_ref, sh_ref, w_ref, xo_ref, p_ref = refs[2 + TOP_K:]
        x = x_ref[...] + gt_ref[0] * _combine(tw_ref, y_refs)
        xo_ref[...] = x
    else:
        x_ref, g_ref, sc_ref, sh_ref, w_ref, p_ref = refs
        x = x_ref[...]
    h = _rms(x, g_ref[...]) * (1.0 + sc_ref[0]) + sh_ref[0]
    p_ref[...] = _mm(h, w_ref[...])


def _mod_spec(dims, chunk):
    d = dims["D"]
    return pl.BlockSpec((1, 1, d), lambda i: (_mod_index(dims, i), 0, chunk))


def _mod_index(dims, i):
    nb = dims["L"] // dims["RB"]
    return jnp.where(i % nb < dims["CT"] // dims["RB"], dims["B"], i // nb)


def _expert_row_specs(rb, d, row_block):
    return [pl.BlockSpec((1, rb, d), functools.partial(lambda k, i: (k, row_block(i), 0), k)) for k in range(TOP_K)]


def _inproj(dims, x, moe, mod_l, mod_prev, g1, w_in_p):
    t, d, rb = dims["T"], dims["D"], dims["RB"]
    np_ = w_in_p.shape[1]
    row = pl.BlockSpec((rb, d), lambda i: (i, 0))
    vec = pl.BlockSpec((1, d), lambda i: (0, 0))
    wspec = pl.BlockSpec((d, np_), lambda i: (0, 0))
    pspec = pl.BlockSpec((rb, np_), lambda i: (i, 0))
    pshape = jax.ShapeDtypeStruct((t, np_), F32)
    if moe is None:
        return x, pl.pallas_call(
            functools.partial(_inproj_kernel, False),
            grid=(t // rb,),
            in_specs=[row, vec, _mod_spec(dims, 1), _mod_spec(dims, 0), wspec],
            out_specs=pspec, out_shape=pshape,
            compiler_params=_cparams("parallel"), name="inproj",
        )(x, g1, mod_l, mod_l, w_in_p)
    ys, tw = moe
    return pl.pallas_call(
        functools.partial(_inproj_kernel, True),
        grid=(t // rb,),
        in_specs=[row, pl.BlockSpec((rb, LANE), lambda i: (i, 0))] + _expert_row_specs(rb, d, lambda i: i)
        + [_mod_spec(dims, 5), vec, _mod_spec(dims, 1), _mod_spec(dims, 0), wspec],
        out_specs=[row, pspec],
        out_shape=[jax.ShapeDtypeStruct((t, d), F32), pshape],
        compiler_params=_cparams("parallel"), name="inproj_res",
    )(x, tw, *([ys] * TOP_K), mod_prev, g1, mod_l, mod_l, w_in_p)


def _mla_prep_kernel(dq_ref, x2_ref, gq_ref, gkv_ref, wq_ref, wk_ref, wv_ref, cos_ref, sin_ref,
                     q_ref, k_ref, v_ref):
    cos = cos_ref[...]
    sin = sin_ref[...]

    def rope(r):
        return r * cos + _swap_pairs(r, MLA_ROPE // 4) * sin

    q = _mm(_rms(dq_ref[...], gq_ref[...]), wq_ref[...])
    x2 = x2_ref[...]
    kvn = _rms(x2[:, :MLA_KV_LORA], gkv_ref[...])
    kn = _mm(kvn, wk_ref[...])
    v = _mm(kvn, wv_ref[...])
    kr = rope(x2[:, MLA_KV_LORA:]).astype(k_ref.dtype)
    ones = jnp.ones((v.shape[0], MLA_VW - MLA_V), v_ref.dtype)
    for h in range(MLA_HEADS):
        o = h * MLA_QK
        q_ref[:, o:o + MLA_NOPE] = q[:, o:o + MLA_NOPE].astype(q_ref.dtype)
        q_ref[:, o + MLA_NOPE:o + MLA_QK] = rope(q[:, o + MLA_NOPE:o + MLA_QK]).astype(q_ref.dtype)
        k_ref[:, o:o + MLA_NOPE] = kn[:, h * MLA_NOPE:(h + 1) * MLA_NOPE].astype(k_ref.dtype)
        k_ref[:, o + MLA_NOPE:o + MLA_QK] = kr
        v_ref[:, h * MLA_VW:h * MLA_VW + MLA_V] = v[:, h * MLA_V:(h + 1) * MLA_V].astype(v_ref.dtype)
        v_ref[:, h * MLA_VW + MLA_V:(h + 1) * MLA_VW] = ones


def _pos_block(dims, i):
    nb = dims["L"] // dims["RB"]
    nc = dims["CT"] // dims["RB"]
    w = i % nb
    return jnp.where(w < nc, dims["S"] // dims["RB"], w - nc)


def _mla_prep(dims, p, g_cq, g_ckv, wq, wk, wv, cos_t, sin_t):
    t, rb = dims["T"], dims["RB"]
    hq = MLA_HEADS * MLA_QK
    hv = MLA_HEADS * MLA_VW
    full = lambda a: pl.BlockSpec(a.shape, lambda i: (0, 0))
    tab = pl.BlockSpec((rb, LANE), lambda i: (_pos_block(dims, i), 0))
    return pl.pallas_call(
        _mla_prep_kernel,
        grid=(t // rb,),
        in_specs=[pl.BlockSpec((rb, 2 * LANE), lambda i: (i, 0)),
                  pl.BlockSpec((rb, 2 * LANE), lambda i: (i, 1)),
                  full(g_cq), full(g_ckv), full(wq), full(wk), full(wv), tab, tab],
        out_specs=[pl.BlockSpec((rb, hq), lambda i: (i, 0)),
                   pl.BlockSpec((rb, hq), lambda i: (i, 0)),
                   pl.BlockSpec((rb, hv), lambda i: (i, 0))],
        out_shape=[jax.ShapeDtypeStruct((t, hq), MXU_DTYPE),
                   jax.ShapeDtypeStruct((t, hq), MXU_DTYPE),
                   jax.ShapeDtypeStruct((t, hv), MXU_DTYPE)],
        compiler_params=_cparams("parallel"), name="mla_prep",
    )(p, p, g_cq, g_ckv, wq, wk, wv, cos_t, sin_t)


_ATTN_EXP2_SCALE = (MLA_NOPE + MLA_ROPE) ** -0.5 * math.log2(math.e)


def _softmax_pv(s, v):
    m = jnp.max(s, -1, keepdims=True)
    e = jnp.exp2(s * _ATTN_EXP2_SCALE - m * _ATTN_EXP2_SCALE)
    pv = _mm(e, v)
    return pv[:, :MLA_V] / pv[:, MLA_V:MLA_V + 1]


def _attn_kernel(n_ctx_blocks, ctx_len, q_ref, k_ref, v_ref, o_ref, sa_ref, sb_ref):
    j = pl.program_id(2)
    n_keys = k_ref.shape[0]

    def step(write_ref, read_ref, keys):
        write_ref[...] = _mm_nt(q_ref[...], k_ref[...])
        if keys:
            o_ref[...] = _softmax_pv(read_ref[:, 0:keys], v_ref[0:keys, :])

    for parity, (write_ref, read_ref) in enumerate(((sa_ref, sb_ref), (sb_ref, sa_ref))):
        @pl.when(jnp.logical_and(j % 2 == parity, j == 0))
        def _():
            step(write_ref, read_ref, 0)

        @pl.when(jnp.logical_and(j % 2 == parity, jnp.logical_and(j >= 1, j <= n_ctx_blocks)))
        def _():
            step(write_ref, read_ref, ctx_len)

        @pl.when(jnp.logical_and(j % 2 == parity, j > n_ctx_blocks))
        def _():
            step(write_ref, read_ref, n_keys)


def _attention(dims, q, k, v):
    b, l, t, rb, ct = dims["B"], dims["L"], dims["T"], dims["RB"], dims["CT"]
    nq = l // rb
    q_blk = lambda bi, j: bi * nq + jnp.minimum(j, nq - 1)
    o_blk = lambda bi, j: bi * nq + jnp.maximum(j - 1, 0)
    return pl.pallas_call(
        functools.partial(_attn_kernel, ct // rb, ct),
        grid=(b, MLA_HEADS, nq + 1),
        in_specs=[pl.BlockSpec((rb, MLA_QK), lambda bi, h, j: (q_blk(bi, j), h)),
                  pl.BlockSpec((l, MLA_QK), lambda bi, h, j: (bi, h)),
                  pl.BlockSpec((l, MLA_VW), lambda bi, h, j: (bi, h))],
        out_specs=pl.BlockSpec((rb, MLA_V), lambda bi, h, j: (o_blk(bi, j), h)),
        out_shape=jax.ShapeDtypeStruct((t, MLA_HEADS * MLA_V), F32),
        scratch_shapes=[pltpu.VMEM((rb, l), F32), pltpu.VMEM((rb, l), F32)],
        compiler_params=_cparams("parallel", "parallel", "arbitrary"), name="mla_attention",
    )(q, k, v)


def _retention_kernel(qf_ref, kf_ref, vf_ref, cosf_ref, sinf_ref, qb_ref, kb_ref, vb_ref, cosb_ref, sinb_ref,
                      dm_ref, wk_ref, wq_ref, cd_ref, of_ref, ob_ref, s_ref):
    @pl.when(pl.program_id(1) == 0)
    def _():
        s_ref[...] = jnp.zeros(s_ref.shape, F32)

    scale = RET_DK ** -0.5
    chains = []
    for h in range(RET_HEADS):
        cols = slice(h * LANE, (h + 1) * LANE)
        for d, (q_ref, k_ref, v_ref, cos_ref, sin_ref, o_ref) in enumerate(
                ((qf_ref, kf_ref, vf_ref, cosf_ref, sinf_ref, of_ref),
                 (qb_ref, kb_ref, vb_ref, cosb_ref, sinb_ref, ob_ref))):
            rope = lambda x: x * cos_ref[...] + _swap_pairs(x, RET_DK // 4) * sin_ref[...]
            c = d * RET_HEADS + h
            chains.append(dict(q=rope(q_ref[:, cols]), k=rope(k_ref[:, cols]) * scale, v=v_ref[:, cols],
                               s=s_ref[c], c=c, o_ref=o_ref, cols=cols))
    for ch in chains:
        ch["sc"] = _mm_nt(ch["q"], ch["k"]) * dm_ref[ch["c"]]
    for ch in chains:
        ch["o"] = _mm(jnp.concatenate([ch["sc"], ch["q"] * wq_ref[ch["c"]]], -1),
                      jnp.concatenate([ch["v"], ch["s"]], 0))
    for ch in chains:
        c = ch["c"]
        ch["s"] = cd_ref[c][0:1, :] * ch["s"] + _mm_tn(ch["k"] * wk_ref[c], ch["v"])
    for ch in chains:
        ch["o_ref"][:, ch["cols"]] = ch["o"]
        s_ref[ch["c"]] = ch["s"]


def _retention_tables(ret_decay):
    c = RET_CHUNK
    lg = jnp.log1p(-jnp.exp(ret_decay.astype(F32))).reshape(2 * RET_HEADS)
    pos = jnp.arange(c, dtype=F32)
    diff = pos[:, None] - pos[None, :]
    lgm = lg[:, None, None]
    dm_f = jnp.where(diff >= 0, jnp.exp(lgm * jnp.maximum(diff, 0.0)), 0.0)
    dm_b = jnp.where(diff <= 0, jnp.exp(lgm * jnp.maximum(-diff, 0.0)), 0.0)
    dm = jnp.concatenate([dm_f[:RET_HEADS], dm_b[RET_HEADS:]], 0)
    bc = lambda a: jnp.broadcast_to(a[:, :, None], (2 * RET_HEADS, c, LANE))
    lg2 = lg[:, None]
    wk = bc(jnp.concatenate([jnp.exp(lg2 * (c - 1.0 - pos)[None, :])[:RET_HEADS],
                             jnp.exp(lg2 * pos[None, :])[RET_HEADS:]], 0))
    wq = bc(jnp.concatenate([jnp.exp(lg2 * (pos + 1.0)[None, :])[:RET_HEADS],
                             jnp.exp(lg2 * (c - pos)[None, :])[RET_HEADS:]], 0))
    cd = jnp.broadcast_to(jnp.exp(lg * c)[:, None, None], (2 * RET_HEADS, 8, LANE))
    return dm, wk, wq, cd


def _retention(dims, p, cos_t, sin_t, tables):
    b, l, t, s, ct = dims["B"], dims["L"], dims["T"], dims["S"], dims["CT"]
    c = RET_CHUNK
    n = l // c
    n_ctx = ct // c
    hw = RET_HEADS * LANE
    dm, wk, wq, cd = tables
    fwd = lambda i: i
    bwd = lambda i: jnp.where(i < n_ctx, n_ctx - 1 - i, n - 1 + n_ctx - i)
    tab = lambda blkf: pl.BlockSpec((c, LANE), lambda bi, i: (jnp.where(blkf(i) < n_ctx, s // c, blkf(i) - n_ctx), 0))
    qkv = lambda blkf: [pl.BlockSpec((c, hw), functools.partial(lambda j, bi, i: (bi * n + blkf(i), j), j))
                        for j in (PB_RET_Q * LANE // hw, PB_RET_K * LANE // hw, PB_RET_V * LANE // hw)]
    full = lambda a: pl.BlockSpec(a.shape, lambda bi, i: (0, 0, 0))
    return pl.pallas_call(
        _retention_kernel,
        grid=(b, n),
        in_specs=qkv(fwd) + [tab(fwd), tab(fwd)] + qkv(bwd) + [tab(bwd), tab(bwd)]
        + [full(dm), full(wk), full(wq), full(cd)],
        out_specs=[pl.BlockSpec((c, hw), lambda bi, i: (bi * n + fwd(i), 0)),
                   pl.BlockSpec((c, hw), lambda bi, i: (bi * n + bwd(i), 0))],
        out_shape=[jax.ShapeDtypeStruct((t, hw), F32), jax.ShapeDtypeStruct((t, hw), F32)],
        scratch_shapes=[pltpu.VMEM((2 * RET_HEADS, RET_DK, LANE), F32)],
        compiler_params=_cparams("parallel", "arbitrary"), name="retention",
    )(p, p, p, cos_t, sin_t, p, p, p, cos_t, sin_t, dm, wk, wq, cd)


def _gdn_conv_kernel(ctx_len, tile, x_ref, w_ref, o_ref):
    l = x_ref.shape[0]
    cb = pl.program_id(1)
    w = w_ref[0]
    half = GDN_CONV // 2
    row = lax.broadcasted_iota(jnp.int32, (tile, LANE), 0)
    for r0 in range(0, l, tile):
        seg0, seg1 = (0, ctx_len) if r0 < ctx_len else (ctx_len, l)
        x = x_ref[r0:r0 + tile, :]
        acc = x * w[half:half + 1, :]
        for d in range(1, half + 1):
            if r0 - d >= seg0:
                back = x_ref[r0 - d:r0 - d + tile, :]
            else:
                back = jnp.where(row >= d, pltpu.roll(x, d, 0), 0.0)
            if r0 + tile + d <= seg1:
                fwd = x_ref[r0 + d:r0 + d + tile, :]
            else:
                fwd = jnp.where(row < tile - d, pltpu.roll(x, tile - d, 0), 0.0)
            acc = acc + back * w[half - d:half - d + 1, :] + fwd * w[half + d:half + d + 1, :]
        y = _silu(acc)
        nrm = lax.rsqrt(jnp.sum(y * y, -1, keepdims=True) + L2_EPS)
        fac = jnp.where(cb < 2 * GDN_HEADS, nrm, 1.0) * jnp.where(cb < GDN_HEADS, GDN_DK ** -0.5, 1.0)
        o_ref[r0:r0 + tile, :] = y * fac


def _gdn_conv(dims, p, conv_w):
    b, l, t, ct, rb = dims["B"], dims["L"], dims["T"], dims["CT"], dims["RB"]
    ncb = 3 * GDN_HEADS
    return pl.pallas_call(
        functools.partial(_gdn_conv_kernel, ct, rb),
        grid=(b, ncb),
        in_specs=[pl.BlockSpec((l, LANE), lambda bi, c: (bi, PB_GDN_Q + c)),
                  pl.BlockSpec((1, 8, LANE), lambda bi, c: (c, 0, 0))],
        out_specs=pl.BlockSpec((l, LANE), lambda bi, c: (bi, c)),
        out_shape=jax.ShapeDtypeStruct((t, ncb * LANE), F32),
        compiler_params=_cparams("parallel", "parallel"), name="gdn_conv",
    )(p, conv_w)


def _gdn_chain_kernel(qf_ref, kf_ref, vf_ref, abf_ref, qb_ref, kb_ref, vb_ref, abb_ref, gp_ref,
                      of_ref, ob_ref, s_ref, pa_ref, da_ref, pb_ref, db_ref):
    t = pl.program_id(1)

    @pl.when(t == 0)
    def _():
        s_ref[...] = jnp.zeros(s_ref.shape, F32)
        pb_ref[...] = jnp.zeros(pb_ref.shape, F32)
        db_ref[...] = jnp.zeros(db_ref.shape, F32)

    @pl.when(t % 2 == 0)
    def _():
        _gdn_chain_step(qf_ref, kf_ref, vf_ref, abf_ref, qb_ref, kb_ref, vb_ref, abb_ref, gp_ref,
                        of_ref, ob_ref, s_ref, pa_ref, da_ref, pb_ref, db_ref)

    @pl.when(t % 2 == 1)
    def _():
        _gdn_chain_step(qf_ref, kf_ref, vf_ref, abf_ref, qb_ref, kb_ref, vb_ref, abb_ref, gp_ref,
                        of_ref, ob_ref, s_ref, pb_ref, db_ref, pa_ref, da_ref)


_GDN_U, _GDN_W, _GDN_ATTN, _GDN_QIN, _GDN_KTAIL = range(5)


def _gdn_chain_step(qf_ref, kf_ref, vf_ref, abf_ref, qb_ref, kb_ref, vb_ref, abb_ref, gp_ref,
                    of_ref, ob_ref, s_ref, pw_ref, dw_ref, pr_ref, dr_ref):
    blk = GDN_BLOCK
    ri = lax.broadcasted_iota(jnp.int32, (blk, blk), 0)
    ci = lax.broadcasted_iota(jnp.int32, (blk, blk), 1)
    same = (ri // GDN_CHUNK) == (ci // GDN_CHUNK)
    sub = [(ri // m) == (ci // m) for m in (GDN_INV_BASE << i for i in range(8)) if m <= GDN_CHUNK]
    eye = jnp.where(ri == ci, 1.0, 0.0)
    lane = lax.broadcasted_iota(jnp.int32, (blk, LANE), 1)
    row = lax.broadcasted_iota(jnp.int32, (blk, LANE), 0)
    row_lo = row < GDN_CHUNK
    pos = row % GDN_CHUNK
    gp = gp_ref[...]

    def gates(ab):
        sp = ab + gp[1:2, :]
        la = -jnp.exp(gp[0:1, :]) * (jnp.maximum(sp, 0.0) + jnp.log1p(jnp.exp(-jnp.abs(sp))))
        return la, jax.nn.sigmoid(ab)

    la_f, beta_f = gates(abf_ref[...])
    la_b, beta_b = gates(abb_ref[...])
    g_f = la_f
    g_b = la_b
    sh = 1
    while sh < GDN_CHUNK:
        g_f = g_f + jnp.where(pos >= sh, pltpu.roll(g_f, sh, 0), 0.0)
        g_b = g_b + jnp.where(pos < GDN_CHUNK - sh, pltpu.roll(g_b, blk - sh, 0), 0.0)
        sh *= 2

    def pick(x, c):
        return jnp.broadcast_to(jnp.sum(jnp.where(lane == c, x, 0.0), -1, keepdims=True), (blk, LANE))

    n_chain = 2 * GDN_HEADS
    is_bwd = lambda c: c >= GDN_HEADS
    ends = lambda c: (0, GDN_CHUNK) if is_bwd(c) else (GDN_CHUNK - 1, blk - 1)

    def prepare():
        chains = []
        for h in range(GDN_HEADS):
            cols = slice(h * LANE, (h + 1) * LANE)
            for d, (q_ref, k_ref, v_ref, g, beta) in enumerate(
                    ((qf_ref, kf_ref, vf_ref, g_f, beta_f), (qb_ref, kb_ref, vb_ref, g_b, beta_b))):
                c = d * GDN_HEADS + h
                chains.append(dict(q=q_ref[:, cols], k=k_ref[:, cols], v=v_ref[:, cols], gb=pick(g, c),
                                   beta=pick(beta, 2 * GDN_HEADS + c), c=c))
        for ch in chains:
            gb = ch["gb"]
            gdiff = gb - gb.T
            incl = same & ((ci >= ri) if is_bwd(ch["c"]) else (ci <= ri))
            ch["strict"] = same & ((ci > ri) if is_bwd(ch["c"]) else (ci < ri))
            ch["decay"] = jnp.where(incl, jnp.exp(jnp.where(incl, gdiff, 0.0)), 0.0)
            ch["kb"] = ch["k"] * ch["beta"]
        yield
        for ch in chains:
            sc = _mm_nt(jnp.concatenate([ch["kb"], ch["q"]], 0), ch["k"])
            ch["a"] = jnp.where(ch["strict"], sc[:blk, :] * ch["decay"], 0.0)
            pw_ref[ch["c"], _GDN_ATTN] = sc[blk:, :] * ch["decay"]
            ch["p"] = -jnp.where(sub[0], ch["a"], 0.0)
            ch["t"] = eye + ch["p"]
        yield
        for ch in chains:
            ch["p"] = _mm(ch["p"], ch["p"])
        yield
        for _ in range(int(math.log2(GDN_INV_BASE)) - 2):
            for ch in chains:
                r = _mm(jnp.concatenate([ch["p"], ch["t"]], 0), ch["p"])
                ch["t"] = ch["t"] + r[blk:, :]
                ch["p"] = r[:blk, :]
            yield
        for ch in chains:
            ch["t"] = ch["t"] + _mm(ch["t"], ch["p"])
        yield
        for lvl in range(1, len(sub)):
            for ch in chains:
                a_off = jnp.where(sub[lvl] & jnp.logical_not(sub[lvl - 1]), ch["a"], 0.0)
                ch["x"] = _mm(a_off, ch["t"])
            yield
            for ch in chains:
                ch["t"] = ch["t"] - _mm(ch["t"], ch["x"])
            yield
        for ch in chains:
            c, gb = ch["c"], ch["gb"]
            eg = jnp.exp(gb)
            uw = _mm(ch["t"], jnp.concatenate([ch["v"] * ch["beta"], ch["kb"] * eg], -1))
            pw_ref[c, _GDN_U] = uw[:, :LANE]
            pw_ref[c, _GDN_W] = uw[:, LANE:]
            pw_ref[c, _GDN_QIN] = ch["q"] * eg
            e0, e1 = ends(c)
            g_end = jnp.where(row_lo, gb[e0:e0 + 1, :], gb[e1:e1 + 1, :])
            pw_ref[c, _GDN_KTAIL] = ch["k"] * jnp.exp(g_end - gb)
            dw_ref[c] = jnp.where(row[:8, :] == 0, jnp.exp(gb[e0:e0 + 1, :]), jnp.exp(gb[e1:e1 + 1, :]))
        yield

    def recur():
        state = [s_ref[c] for c in range(n_chain)]
        outs = [[None, None] for _ in range(n_chain)]
        vn = [None] * n_chain
        for step in range(2):
            chunk = lambda c: (1 - step) if is_bwd(c) else step
            rows = lambda c: slice(chunk(c) * GDN_CHUNK, (chunk(c) + 1) * GDN_CHUNK)
            for c in range(n_chain):
                v_new = pr_ref[c, _GDN_U, rows(c), :] - _mm(pr_ref[c, _GDN_W, rows(c), :], state[c])
                zero = jnp.zeros_like(v_new)
                vn[c] = jnp.concatenate([v_new, zero] if chunk(c) == 0 else [zero, v_new], 0)
            yield
            for c in range(n_chain):
                outs[c][chunk(c)] = _mm(
                    jnp.concatenate([pr_ref[c, _GDN_ATTN, rows(c), :], pr_ref[c, _GDN_QIN, rows(c), :]], -1),
                    jnp.concatenate([vn[c], state[c]], 0))
            yield
            for c in range(n_chain):
                in_c = row_lo if chunk(c) == 0 else jnp.logical_not(row_lo)
                dec = dr_ref[c][chunk(c):chunk(c) + 1, :]
                state[c] = state[c] * dec + _mm_tn(jnp.where(in_c, pr_ref[c, _GDN_KTAIL], 0.0), vn[c])
            yield
        for c in range(n_chain):
            o_ref = ob_ref if is_bwd(c) else of_ref
            h = c % GDN_HEADS
            o_ref[:, h * LANE:(h + 1) * LANE] = jnp.concatenate(outs[c], 0)
            s_ref[c] = state[c]
        yield

    live = [prepare(), recur()]
    while live:
        for gen in list(live):
            if next(gen, "done") == "done":
                live.remove(gen)


def _gdn_chain(dims, p, qkv, gate_params):
    b, l, t, ct = dims["B"], dims["L"], dims["T"], dims["CT"]
    blk = GDN_BLOCK
    n = l // blk
    n_ctx = ct // blk
    hw = GDN_HEADS * LANE
    fwd = lambda bi, i: bi * n + i
    bwd = lambda bi, i: bi * n + jnp.where(i < n_ctx, n_ctx - 1 - i, n - 1 + n_ctx - i)
    prep = lambda rowf: (lambda bi, i: rowf(bi, jnp.minimum(i, n - 1)))
    done = lambda rowf: (lambda bi, i: rowf(bi, jnp.maximum(i - 1, 0)))
    qkv_specs = lambda rowf: [pl.BlockSpec((blk, hw), functools.partial(lambda j, bi, i: (rowf(bi, i), j), j))
                              for j in range(3)]
    ab_spec = lambda rowf: pl.BlockSpec((blk, LANE), lambda bi, i: (rowf(bi, i), PB_GDN_AB))
    prepared = pltpu.VMEM((2 * GDN_HEADS, 5, blk, LANE), F32)
    decays = pltpu.VMEM((2 * GDN_HEADS, 8, LANE), F32)
    return pl.pallas_call(
        _gdn_chain_kernel,
        grid=(b, n + 1),
        in_specs=qkv_specs(prep(fwd)) + [ab_spec(prep(fwd))] + qkv_specs(prep(bwd)) + [ab_spec(prep(bwd))]
        + [pl.BlockSpec((8, LANE), lambda bi, i: (0, 0))],
        out_specs=[pl.BlockSpec((blk, hw), lambda bi, i: (done(fwd)(bi, i), 0)),
                   pl.BlockSpec((blk, hw), lambda bi, i: (done(bwd)(bi, i), 0))],
        out_shape=[jax.ShapeDtypeStruct((t, hw), F32), jax.ShapeDtypeStruct((t, hw), F32)],
        scratch_shapes=[pltpu.VMEM((2 * GDN_HEADS, GDN_DK, LANE), F32), prepared, decays, prepared, decays],
        compiler_params=_cparams("parallel", "arbitrary"), name="gdn_chain",
    )(qkv, qkv, qkv, p, qkv, qkv, qkv, p, gate_params)


def _outproj_kernel(n_experts, om_ref, rf_ref, rb_ref, rg_ref, gf_ref, gb_ref, z_ref, x_ref, gm_ref, gr_ref,
                    gg_ref, w_ref, gt_ref, g2_ref, sc_ref, sh_ref, wr_ref, br_ref, xo_ref, hp_ref, ti_ref, tw_ref):
    ret, gdn = [], []
    for h in range(RET_HEADS):
        cols = slice(h * LANE, (h + 1) * LANE)
        o = rf_ref[:, cols] + rb_ref[:, cols]
        oc = o - jnp.mean(o, -1, keepdims=True)
        var = jnp.mean(oc * oc, -1, keepdims=True)
        ret.append(oc * lax.rsqrt(var + GROUP_NORM_EPS) * gr_ref[:, cols] * _silu(rg_ref[:, cols]))
    for h in range(GDN_HEADS):
        cols = slice(h * LANE, (h + 1) * LANE)
        o = gf_ref[:, cols] + gb_ref[:, cols]
        gdn.append(_rms(o, gg_ref[...]) * _silu(z_ref[:, cols]))
    mixed = jnp.concatenate([_rms(om_ref[...], gm_ref[...])] + ret + gdn, -1)
    x = x_ref[...] + gt_ref[0] * _mm(mixed, w_ref[...])
    xo_ref[...] = x
    h = (_rms(x, g2_ref[...]) * (1.0 + sc_ref[0]) + sh_ref[0]).astype(MXU_DTYPE)
    hp_ref[...] = h.astype(F32)
    logits = _mm(h, wr_ref[...]) + br_ref[...]
    lane = lax.broadcasted_iota(jnp.int32, logits.shape, 1)
    lane_f = lane.astype(F32)
    neg = jnp.float32(-jnp.inf)
    logits = jnp.where(lane < n_experts, logits, neg)
    vals, idxs = [], []
    for _ in range(TOP_K):
        m = jnp.max(logits, -1, keepdims=True)
        idx = jnp.min(jnp.where(logits == m, lane_f, float(LANE)), -1, keepdims=True)
        logits = jnp.where(lane_f == idx, neg, logits)
        vals.append(m)
        idxs.append(idx.astype(jnp.int32))
    es = [jnp.exp(v - vals[0]) for v in vals]
    den = es[0]
    for e in es[1:]:
        den = den + e
    ti = jnp.zeros(logits.shape, jnp.int32)
    tw = jnp.zeros(logits.shape, F32)
    for k in range(TOP_K):
        ti = jnp.where(lane == k, idxs[k], ti)
        tw = jnp.where(lane == k, es[k] / den, tw)
    ti_ref[...] = ti
    tw_ref[...] = tw


def _outproj(dims, om, rf, rbk, gf, gb, p, x, mod_l, g_mla, g_ret, g_gdn, w_out, g2, w_router_p, b_router_p,
             n_experts):
    t, d, rb = dims["T"], dims["D"], dims["RB"]
    hw = GDN_HEADS * LANE
    row = lambda w: pl.BlockSpec((rb, w), lambda i: (i, 0))
    full = lambda a: pl.BlockSpec(a.shape, lambda i: (0, 0))
    pcol = lambda blk: pl.BlockSpec((rb, hw), lambda i: (i, blk * LANE // hw))
    return pl.pallas_call(
        functools.partial(_outproj_kernel, n_experts),
        grid=(t // rb,),
        in_specs=[row(om.shape[1]), row(hw), row(hw), pcol(PB_RET_G), row(hw), row(hw), pcol(PB_GDN_Z), row(d),
                  full(g_mla), full(g_ret), full(g_gdn), full(w_out),
                  _mod_spec(dims, 2), full(g2), _mod_spec(dims, 4), _mod_spec(dims, 3),
                  full(w_router_p), full(b_router_p)],
        out_specs=[row(d), row(d), row(LANE), row(LANE)],
        out_shape=[jax.ShapeDtypeStruct((t, d), F32), jax.ShapeDtypeStruct((t, d), F32),
                   jax.ShapeDtypeStruct((t, LANE), jnp.int32), jax.ShapeDtypeStruct((t, LANE), F32)],
        compiler_params=_cparams("parallel"), name="outproj_router",
    )(om, rf, rbk, p, gf, gb, p, x, g_mla, g_ret, g_gdn, w_out, mod_l, g2, mod_l, mod_l, w_router_p, b_router_p)


def _expert_kernel(be_ref, nu_ref, x_ref, wgu_ref, bgu_ref, wdn_ref, bdn_ref, y_ref, wgu_s, wdn_s):
    i = pl.program_id(0)
    f = wdn_ref.shape[2]
    prev = be_ref[jnp.maximum(i - 1, 0)]

    @pl.when(jnp.logical_or(i == 0, be_ref[i] != prev))
    def _():
        wgu_s[...] = wgu_ref[0, 0].astype(wgu_s.dtype)
        wdn_s[...] = wdn_ref[0, 0].astype(wdn_s.dtype)

    @pl.when(i < nu_ref[0])
    def _():
        gu = _mm(x_ref[...], wgu_s[...]) + bgu_ref[0]
        gate = jnp.minimum(gu[:, :f], SWIGLU_LIMIT)
        up = jnp.clip(gu[:, f:], -SWIGLU_LIMIT, SWIGLU_LIMIT)
        act = (up + 1.0) * (gate * jax.nn.sigmoid(gate * SWIGLU_ALPHA))
        y_ref[...] = _mm(act, wdn_s[...]) + bdn_ref[0]

    @pl.when(i >= nu_ref[0])
    def _():
        y_ref[...] = jnp.zeros(y_ref.shape, F32)


def _experts(li, xs, block_e, n_used, w_gu, b_gu, w_dn, b_dn):
    rows = xs.shape[0]
    depth, e, d, f2 = w_gu.shape
    f = f2 // 2
    nblk = rows // MOE_BLOCK
    grid_spec = pltpu.PrefetchScalarGridSpec(
        num_scalar_prefetch=2,
        grid=(nblk,),
        in_specs=[pl.BlockSpec((MOE_BLOCK, d), lambda i, be, nu: (i, 0)),
                  pl.BlockSpec((1, 1, d, f2), lambda i, be, nu: (li, be[i], 0, 0)),
                  pl.BlockSpec((1, 1, f2), lambda i, be, nu: (li * e + be[i], 0, 0)),
                  pl.BlockSpec((1, 1, f, d), lambda i, be, nu: (li, be[i], 0, 0)),
                  pl.BlockSpec((1, 1, d), lambda i, be, nu: (li * e + be[i], 0, 0))],
        out_specs=pl.BlockSpec((MOE_BLOCK, d), lambda i, be, nu: (i, 0)),
        scratch_shapes=[pltpu.VMEM((d, f2), MXU_DTYPE), pltpu.VMEM((f, d), MXU_DTYPE)],
    )
    return pl.pallas_call(
        _expert_kernel,
        grid_spec=grid_spec,
        out_shape=jax.ShapeDtypeStruct((rows, d), F32),
        compiler_params=_cparams("arbitrary"), name="experts",
    )(block_e, n_used, xs, w_gu, b_gu.reshape(depth * e, 1, f2), w_dn, b_dn.reshape(depth * e, 1, d))


def _moe(li, hp, top_i, w_gu, b_gu, w_dn, b_dn):
    t = hp.shape[0]
    e = w_gu.shape[1]
    idx = top_i[:, :TOP_K]
    onehot = idx[:, :, None] == jnp.arange(e, dtype=jnp.int32)[None, None, :]
    member = jnp.sum(onehot, 1).astype(jnp.int32)
    counts = jnp.sum(member, 0)
    rank = jnp.cumsum(member, 0) - member
    padded = (counts + MOE_BLOCK - 1) // MOE_BLOCK * MOE_BLOCK
    pend = jnp.cumsum(padded)
    pstart = pend - padded
    dest = jnp.sum(jnp.where(onehot, (rank + pstart[None, :])[:, None, :], 0), -1)
    nblk = -(-(t * TOP_K) // MOE_BLOCK) + e
    rows = nblk * MOE_BLOCK
    tok = jnp.broadcast_to(jnp.arange(t, dtype=jnp.int32)[:, None], (t, TOP_K))
    row_tok = jnp.zeros((rows,), jnp.int32).at[dest.reshape(-1)].set(tok.reshape(-1), unique_indices=True)
    xs = hp[row_tok]
    blk_row = jnp.arange(nblk, dtype=jnp.int32) * MOE_BLOCK
    block_e = jnp.minimum(jnp.sum(pend[None, :] <= blk_row[:, None], 1), e - 1).astype(jnp.int32)
    n_used = (pend[-1] // MOE_BLOCK).astype(jnp.int32).reshape(1)
    ys = _experts(li, xs, block_e, n_used, w_gu, b_gu, w_dn, b_dn)
    return ys[dest.T.reshape(-1)].reshape(TOP_K, t, ys.shape[1])


def _final_kernel(x_ref, tw_ref, *refs):
    y_refs = refs[:TOP_K]
    gt_ref, g_ref, o_ref = refs[TOP_K:]
    o_ref[...] = _rms(x_ref[...] + gt_ref[0] * _combine(tw_ref, y_refs), g_ref[...])


def _final(dims, x, moe, mod_l, g_final):
    b, s, d, rb, l, ct = dims["B"], dims["S"], dims["D"], dims["RB"], dims["L"], dims["CT"]
    ys, tw = moe
    ns = s // rb
    src = lambda i: (i // ns) * (l // rb) + ct // rb + i % ns
    return pl.pallas_call(
        _final_kernel,
        grid=(b * ns,),
        in_specs=[pl.BlockSpec((rb, d), lambda i: (src(i), 0)), pl.BlockSpec((rb, LANE), lambda i: (src(i), 0))]
        + _expert_row_specs(rb, d, src)
        + [pl.BlockSpec((1, 1, d), lambda i: (i // ns, 0, 5)), pl.BlockSpec((1, d), lambda i: (0, 0))],
        out_specs=pl.BlockSpec((rb, d), lambda i: (i, 0)),
        out_shape=jax.ShapeDtypeStruct((b * s, d), F32),
        compiler_params=_cparams("parallel"), name="final_norm",
    )(x, tw, *([ys] * TOP_K), mod_l, g_final)


def _rope_tables(s, rb, quarter, width):
    rows = (jnp.arange(s, dtype=jnp.int32) // GRID_W).astype(F32)
    cols = (jnp.arange(s, dtype=jnp.int32) % GRID_W).astype(F32)
    inv = ROPE_BASE ** (-jnp.arange(quarter, dtype=F32) / quarter)
    ar = rows[:, None] * inv[None, :]
    ac = cols[:, None] * inv[None, :]
    pad = width - 4 * quarter
    cos = jnp.concatenate([jnp.cos(ar), jnp.cos(ar), jnp.cos(ac), jnp.cos(ac), jnp.ones((s, pad), F32)], -1)
    sin = jnp.concatenate([-jnp.sin(ar), jnp.sin(ar), -jnp.sin(ac), jnp.sin(ac), jnp.zeros((s, pad), F32)], -1)
    cos = jnp.concatenate([cos, jnp.ones((rb, width), F32)], 0)
    sin = jnp.concatenate([sin, jnp.zeros((rb, width), F32)], 0)
    return cos, sin


def _layout_w_in(w):
    d = w.shape[0]
    n_mla = MLA_Q_LORA + MLA_KV_LORA + MLA_ROPE
    n_mid = 4 * RET_HEADS * RET_DK + 4 * GDN_HEADS * GDN_DK
    z = lambda n: jnp.zeros((d, n), w.dtype)
    out = jnp.concatenate([w[:, :n_mla], z(PB_RET_Q * LANE - n_mla), w[:, n_mla:n_mla + n_mid],
                           w[:, n_mla + n_mid:], z(LANE - 4 * GDN_HEADS)], -1)
    assert out.shape[1] == P_BLOCKS * LANE
    return out.astype(MXU_DTYPE)


def _layout_w_uq(w):
    r = w.shape[0]
    w = w.reshape(r, MLA_HEADS, MLA_NOPE + MLA_ROPE)
    w = jnp.concatenate([w, jnp.zeros((r, MLA_HEADS, MLA_QK - MLA_NOPE - MLA_ROPE), w.dtype)], -1)
    return w.reshape(r, MLA_HEADS * MLA_QK).astype(MXU_DTYPE)


def _layout_w_ukv(w):
    r = w.shape[0]
    w = w.reshape(r, MLA_HEADS, MLA_NOPE + MLA_V)
    wk = w[:, :, :MLA_NOPE].reshape(r, MLA_HEADS * MLA_NOPE)
    wv = w[:, :, MLA_NOPE:].reshape(r, MLA_HEADS * MLA_V)
    return wk.astype(MXU_DTYPE), wv.astype(MXU_DTYPE)


def kernel(x, c, ctx, c_ctx, w_mod, b_mod, g_norm1, g_norm2, w_in, g_cq, g_ckv, w_uq, w_ukv, g_mla, ret_decay, g_ret, gdn_conv, gdn_a_log, gdn_dt_bias, g_gdn, w_out, w_router, b_router, w_gate_up, b_gate_up, w_down, b_down, g_final):
    b, s, d = x.shape
    ct = ctx.shape[1]
    depth = w_mod.shape[0]
    n_experts = w_router.shape[-1]
    rb = math.gcd(ROW_BLOCK, ct)
    l = ct + s
    dims = dict(B=b, S=s, D=d, CT=ct, L=l, T=b * l, RB=rb)
    assert b + 1 <= 8 and ct % GDN_BLOCK == 0 and s % rb == 0 and n_experts <= LANE

    stream = jnp.concatenate([ctx, x], 1).reshape(b * l, d)
    cs = jnp.concatenate([c, c_ctx[None, :], jnp.zeros((7 - b, d), F32)], 0)
    mod = _modulation(cs, w_mod, b_mod).reshape(depth, 8, 1, 6 * d)

    cos_m, sin_m = _rope_tables(s, rb, MLA_ROPE // 4, LANE)
    cos_r, sin_r = _rope_tables(s, RET_CHUNK, RET_DK // 4, LANE)
    moe = None
    mod_prev = None
    for li in range(depth):
        w_in_p = _layout_w_in(w_in[li])
        stream, p = _inproj(dims, stream, moe, mod[li], mod_prev, g_norm1[li][None, :], w_in_p)
        wk, wv = _layout_w_ukv(w_ukv[li])
        q_m, k_m, v_m = _mla_prep(dims, p, g_cq[li][None, :], g_ckv[li][None, :], _layout_w_uq(w_uq[li]),
                                  wk, wv, cos_m, sin_m)
        o_mla = _attention(dims, q_m, k_m, v_m)
        o_rf, o_rb = _retention(dims, p, cos_r, sin_r, _retention_tables(ret_decay[li]))
        conv_w = jnp.zeros((8, 3 * GDN_HEADS * LANE), F32).at[:GDN_CONV].set(gdn_conv[li])
        conv_w = conv_w.reshape(8, 3 * GDN_HEADS, LANE).transpose(1, 0, 2)
        gate_params = jnp.zeros((8, LANE), F32)
        gate_params = gate_params.at[0, :2 * GDN_HEADS].set(gdn_a_log[li].reshape(-1))
        gate_params = gate_params.at[1, :2 * GDN_HEADS].set(gdn_dt_bias[li].reshape(-1))
        o_gf, o_gb = _gdn_chain(dims, p, _gdn_conv(dims, p, conv_w), gate_params)
        w_router_p = jnp.zeros((d, LANE), F32).at[:, :n_experts].set(w_router[li]).astype(MXU_DTYPE)
        b_router_p = jnp.zeros((1, LANE), F32).at[0, :n_experts].set(b_router[li])
        stream, hp, top_i, top_w = _outproj(dims, o_mla, o_rf, o_rb, o_gf, o_gb, p, stream, mod[li],
                                            g_mla[li][None, :], g_ret[li][None, :], g_gdn[li][None, :],
                                            w_out[li].astype(MXU_DTYPE),
                                            g_norm2[li][None, :], w_router_p, b_router_p, n_experts)
        moe = (_moe(li, hp, top_i, w_gate_up, b_gate_up, w_down, b_down), top_w)
        mod_prev = mod[li]
    return _final(dims, stream, moe, mod_prev, g_final[None, :]).reshape(b, s, d)
```

```python
import functools
import math

import jax
import jax.numpy as jnp
from jax import lax
from jax.experimental import pallas as pl
from jax.experimental.pallas import tpu as pltpu

F32 = jnp.float32
MXU_DTYPE = jnp.bfloat16

LANE = 128
LANE_BITS = LANE.bit_length() - 1
V7X_VMEM_LIMIT_BYTES = 56 * 1024 * 1024

MLA_HEADS = 4
MLA_NOPE = 128
MLA_ROPE = 64
MLA_V = 128
MLA_Q_LORA = 256
MLA_KV_LORA = 128
MLA_QK = 256
MLA_VW = 256
RET_HEADS = 4
RET_DK = 128
GDN_HEADS = 4
GDN_DK = 128
GDN_CONV = 5
GDN_CHUNK = 64
GDN_BLOCK = 2 * GDN_CHUNK
GDN_INV_BASE = 16
RET_CHUNK = 128
TOP_K = 4
SWIGLU_LIMIT = 7.0
SWIGLU_ALPHA = 1.702
MOE_BLOCK = 256
GRID_W = 64
ROPE_BASE = 10000.0
NORM_EPS = 1e-6
GROUP_NORM_EPS = 1e-5
L2_EPS = 1e-6
ROW_BLOCK = 256

PB_DQ = 0
PB_DKV = 2
PB_RET_Q, PB_RET_K, PB_RET_V, PB_RET_G = 4, 8, 12, 16
PB_GDN_Q, PB_GDN_K, PB_GDN_V, PB_GDN_Z = 20, 24, 28, 32
PB_GDN_AB = 36
P_BLOCKS = 37


def _cparams(*sem):
    return pltpu.CompilerParams(dimension_semantics=sem, vmem_limit_bytes=V7X_VMEM_LIMIT_BYTES)


def _mm(a, b):
    return jnp.dot(a.astype(MXU_DTYPE), b.astype(MXU_DTYPE), preferred_element_type=F32)


def _mm_nt(a, b):
    return lax.dot_general(a.astype(MXU_DTYPE), b.astype(MXU_DTYPE), (((1,), (1,)), ((), ())),
                           preferred_element_type=F32)


def _mm_tn(a, b):
    return jnp.dot(a.T.astype(MXU_DTYPE), b.astype(MXU_DTYPE), preferred_element_type=F32)


def _rms(x, g, eps=NORM_EPS):
    return x * lax.rsqrt(jnp.mean(x * x, -1, keepdims=True) + eps) * g


def _silu(x):
    return x * jax.nn.sigmoid(x)


def _swap_pairs(x, q):
    n = x.shape[-1]
    lane = lax.broadcasted_iota(jnp.int32, x.shape, x.ndim - 1)
    first = (lane % (2 * q)) < q
    return jnp.where(first, pltpu.roll(x, n - q, x.ndim - 1), pltpu.roll(x, q, x.ndim - 1))


def _mod_kernel(c_ref, w_ref, b_ref, o_ref):
    o_ref[0] = _mm(_silu(c_ref[...]), w_ref[0]) + b_ref[0]


def _modulation(cs, w_mod, b_mod):
    depth, d, n = w_mod.shape
    tn = n // 6
    return pl.pallas_call(
        _mod_kernel,
        grid=(depth, n // tn),
        in_specs=[pl.BlockSpec((8, d), lambda l, j: (0, 0)),
                  pl.BlockSpec((1, d, tn), lambda l, j: (l, 0, j)),
                  pl.BlockSpec((1, 1, tn), lambda l, j: (l, 0, j))],
        out_specs=pl.BlockSpec((1, 8, tn), lambda l, j: (l, 0, j)),
        out_shape=jax.ShapeDtypeStruct((depth, 8, n), F32),
        compiler_params=_cparams("parallel", "parallel"),
        name="modulation",
    )(cs, w_mod, b_mod.reshape(depth, 1, n))


def _combine(tw_ref, y_refs):
    tw = tw_ref[...]
    y = y_refs[0][0] * tw[:, 0:1]
    for k in range(1, TOP_K):
        y = y + y_refs[k][0] * tw[:, k:k + 1]
    return y


def _inproj_kernel(has_y, *refs):
    if has_y:
        x_ref, tw_ref = refs[0], refs[1]
        y_refs = refs[2:2 + TOP_K]
        gt_ref, g_ref, sc_ref, sh_ref, w_ref, xo_ref, p_ref = refs[2 + TOP_K:]
        x = x_ref[...] + gt_ref[0] * _combine(tw_ref, y_refs)
        xo_ref[...] = x
    else:
        x_ref, g_ref, sc_ref, sh_ref, w_ref, p_ref = refs
        x = x_ref[...]
    h = _rms(x, g_ref[...]) * (1.0 + sc_ref[0]) + sh_ref[0]
    p_ref[...] = _mm(h, w_ref[...])


def _mod_spec(dims, chunk):
    d = dims["D"]
    return pl.BlockSpec((1, 1, d), lambda i: (_mod_index(dims, i), 0, chunk))


def _mod_index(dims, i):
    nb = dims["L"] // dims["RB"]
    return jnp.where(i % nb < dims["CT"] // dims["RB"], dims["B"], i // nb)


def _expert_row_specs(rb, d, row_block):
    return [pl.BlockSpec((1, rb, d), functools.partial(lambda k, i: (k, row_block(i), 0), k)) for k in range(TOP_K)]


def _inproj(dims, x, moe, mod_l, mod_prev, g1, w_in_p):
    t, d, rb = dims["T"], dims["D"], dims["RB"]
    np_ = w_in_p.shape[1]
    row = pl.BlockSpec((rb, d), lambda i: (i, 0))
    vec = pl.BlockSpec((1, d), lambda i: (0, 0))
    wspec = pl.BlockSpec((d, np_), lambda i: (0, 0))
    pspec = pl.BlockSpec((rb, np_), lambda i: (i, 0))
    pshape = jax.ShapeDtypeStruct((t, np_), F32)
    if moe is None:
        return x, pl.pallas_call(
            functools.partial(_inproj_kernel, False),
            grid=(t // rb,),
            in_specs=[row, vec, _mod_spec(dims, 1), _mod_spec(dims, 0), wspec],
            out_specs=pspec, out_shape=pshape,
            compiler_params=_cparams("parallel"), name="inproj",
        )(x, g1, mod_l, mod_l, w_in_p)
    ys, tw = moe
    return pl.pallas_call(
        functools.partial(_inproj_kernel, True),
        grid=(t // rb,),
        in_specs=[row, pl.BlockSpec((rb, LANE), lambda i: (i, 0))] + _expert_row_specs(rb, d, lambda i: i)
        + [_mod_spec(dims, 5), vec, _mod_spec(dims, 1), _mod_spec(dims, 0), wspec],
        out_specs=[row, pspec],
        out_shape=[jax.ShapeDtypeStruct((t, d), F32), pshape],
        compiler_params=_cparams("parallel"), name="inproj_res",
    )(x, tw, *([ys] * TOP_K), mod_prev, g1, mod_l, mod_l, w_in_p)


def _mla_prep_kernel(dq_ref, x2_ref, gq_ref, gkv_ref, wq_ref, wk_ref, wv_ref, cos_ref, sin_ref,
                     q_ref, k_ref, v_ref):
    cos = cos_ref[...]
    sin = sin_ref[...]

    def rope(r):
        return r * cos + _swap_pairs(r, MLA_ROPE // 4) * sin

    q = _mm(_rms(dq_ref[...], gq_ref[...]), wq_ref[...])
    x2 = x2_ref[...]
    kvn = _rms(x2[:, :MLA_KV_LORA], gkv_ref[...])
    kn = _mm(kvn, wk_ref[...])
    v = _mm(kvn, wv_ref[...])
    kr = rope(x2[:, MLA_KV_LORA:]).astype(k_ref.dtype)
    ones = jnp.ones((v.shape[0], MLA_VW - MLA_V), v_ref.dtype)
    for h in range(MLA_HEADS):
        o = h * MLA_QK
        q_ref[:, o:o + MLA_NOPE] = q[:, o:o + MLA_NOPE].astype(q_ref.dtype)
        q_ref[:, o + MLA_NOPE:o + MLA_QK] = rope(q[:, o + MLA_NOPE:o + MLA_QK]).astype(q_ref.dtype)
        k_ref[:, o:o + MLA_NOPE] = kn[:, h * MLA_NOPE:(h + 1) * MLA_NOPE].astype(k_ref.dtype)
        k_ref[:, o + MLA_NOPE:o + MLA_QK] = kr
        v_ref[:, h * MLA_VW:h * MLA_VW + MLA_V] = v[:, h * MLA_V:(h + 1) * MLA_V].astype(v_ref.dtype)
        v_ref[:, h * MLA_VW + MLA_V:(h + 1) * MLA_VW] = ones


def _pos_block(dims, i):
    nb = dims["L"] // dims["RB"]
    nc = dims["CT"] // dims["RB"]
    w = i % nb
    return jnp.where(w < nc, dims["S"] // dims["RB"], w - nc)


def _mla_prep(dims, p, g_cq, g_ckv, wq, wk, wv, cos_t, sin_t):
    t, rb = dims["T"], dims["RB"]
    hq = MLA_HEADS * MLA_QK
    hv = MLA_HEADS * MLA_VW
    full = lambda a: pl.BlockSpec(a.shape, lambda i: (0, 0))
    tab = pl.BlockSpec((rb, LANE), lambda i: (_pos_block(dims, i), 0))
    return pl.pallas_call(
        _mla_prep_kernel,
        grid=(t // rb,),
        in_specs=[pl.BlockSpec((rb, 2 * LANE), lambda i: (i, 0)),
                  pl.BlockSpec((rb, 2 * LANE), lambda i: (i, 1)),
                  full(g_cq), full(g_ckv), full(wq), full(wk), full(wv), tab, tab],
        out_specs=[pl.BlockSpec((rb, hq), lambda i: (i, 0)),
                   pl.BlockSpec((rb, hq), lambda i: (i, 0)),
                   pl.BlockSpec((rb, hv), lambda i: (i, 0))],
        out_shape=[jax.ShapeDtypeStruct((t, hq), MXU_DTYPE),
                   jax.ShapeDtypeStruct((t, hq), MXU_DTYPE),
                   jax.ShapeDtypeStruct((t, hv), MXU_DTYPE)],
        compiler_params=_cparams("parallel"), name="mla_prep",
    )(p, p, g_cq, g_ckv, wq, wk, wv, cos_t, sin_t)


_ATTN_EXP2_SCALE = (MLA_NOPE + MLA_ROPE) ** -0.5 * math.log2(math.e)


def _softmax_pv(s, v):
    m = jnp.max(s, -1, keepdims=True)
    e = jnp.exp2(s * _ATTN_EXP2_SCALE - m * _ATTN_EXP2_SCALE)
    pv = _mm(e, v)
    return pv[:, :MLA_V] / pv[:, MLA_V:MLA_V + 1]


def _attn_kernel(n_ctx_blocks, ctx_len, q_ref, k_ref, v_ref, o_ref, sa_ref, sb_ref):
    j = pl.program_id(2)
    n_keys = k_ref.shape[0]

    def step(write_ref, read_ref, keys):
        write_ref[...] = _mm_nt(q_ref[...], k_ref[...])
        if keys:
            o_ref[...] = _softmax_pv(read_ref[:, 0:keys], v_ref[0:keys, :])

    for parity, (write_ref, read_ref) in enumerate(((sa_ref, sb_ref), (sb_ref, sa_ref))):
        @pl.when(jnp.logical_and(j % 2 == parity, j == 0))
        def _():
            step(write_ref, read_ref, 0)

        @pl.when(jnp.logical_and(j % 2 == parity, jnp.logical_and(j >= 1, j <= n_ctx_blocks)))
        def _():
            step(write_ref, read_ref, ctx_len)

        @pl.when(jnp.logical_and(j % 2 == parity, j > n_ctx_blocks))
        def _():
            step(write_ref, read_ref, n_keys)


def _attention(dims, q, k, v):
    b, l, t, rb, ct = dims["B"], dims["L"], dims["T"], dims["RB"], dims["CT"]
    nq = l // rb
    q_blk = lambda bi, j: bi * nq + jnp.minimum(j, nq - 1)
    o_blk = lambda bi, j: bi * nq + jnp.maximum(j - 1, 0)
    return pl.pallas_call(
        functools.partial(_attn_kernel, ct // rb, ct),
        grid=(b, MLA_HEADS, nq + 1),
        in_specs=[pl.BlockSpec((rb, MLA_QK), lambda bi, h, j: (q_blk(bi, j), h)),
                  pl.BlockSpec((l, MLA_QK), lambda bi, h, j: (bi, h)),
                  pl.BlockSpec((l, MLA_VW), lambda bi, h, j: (bi, h))],
        out_specs=pl.BlockSpec((rb, MLA_V), lambda bi, h, j: (o_blk(bi, j), h)),
        out_shape=jax.ShapeDtypeStruct((t, MLA_HEADS * MLA_V), F32),
        scratch_shapes=[pltpu.VMEM((rb, l), F32), pltpu.VMEM((rb, l), F32)],
        compiler_params=_cparams("parallel", "parallel", "arbitrary"), name="mla_attention",
    )(q, k, v)


def _retention_kernel(qf_ref, kf_ref, vf_ref, cosf_ref, sinf_ref, qb_ref, kb_ref, vb_ref, cosb_ref, sinb_ref,
                      dm_ref, wk_ref, wq_ref, cd_ref, of_ref, ob_ref, s_ref):
    @pl.when(pl.program_id(1) == 0)
    def _():
        s_ref[...] = jnp.zeros(s_ref.shape, F32)

    scale = RET_DK ** -0.5
    chains = []
    for h in range(RET_HEADS):
        cols = slice(h * LANE, (h + 1) * LANE)
        for d, (q_ref, k_ref, v_ref, cos_ref, sin_ref, o_ref) in enumerate(
                ((qf_ref, kf_ref, vf_ref, cosf_ref, sinf_ref, of_ref),
                 (qb_ref, kb_ref, vb_ref, cosb_ref, sinb_ref, ob_ref))):
            rope = lambda x: x * cos_ref[...] + _swap_pairs(x, RET_DK // 4) * sin_ref[...]
            c = d * RET_HEADS + h
            chains.append(dict(q=rope(q_ref[:, cols]), k=rope(k_ref[:, cols]) * scale, v=v_ref[:, cols],
                               s=s_ref[c], c=c, o_ref=o_ref, cols=cols))
    for ch in chains:
        ch["sc"] = _mm_nt(ch["q"], ch["k"]) * dm_ref[ch["c"]]
    for ch in chains:
        ch["o"] = _mm(jnp.concatenate([ch["sc"], ch["q"] * wq_ref[ch["c"]]], -1),
                      jnp.concatenate([ch["v"], ch["s"]], 0))
    for ch in chains:
        c = ch["c"]
        ch["s"] = cd_ref[c][0:1, :] * ch["s"] + _mm_tn(ch["k"] * wk_ref[c], ch["v"])
    for ch in chains:
        ch["o_ref"][:, ch["cols"]] = ch["o"]
        s_ref[ch["c"]] = ch["s"]


def _retention_tables(ret_decay):
    c = RET_CHUNK
    lg = jnp.log1p(-jnp.exp(ret_decay.astype(F32))).reshape(2 * RET_HEADS)
    pos = jnp.arange(c, dtype=F32)
    diff = pos[:, None] - pos[None, :]
    lgm = lg[:, None, None]
    dm_f = jnp.where(diff >= 0, jnp.exp(lgm * jnp.maximum(diff, 0.0)), 0.0)
    dm_b = jnp.where(diff <= 0, jnp.exp(lgm * jnp.maximum(-diff, 0.0)), 0.0)
    dm = jnp.concatenate([dm_f[:RET_HEADS], dm_b[RET_HEADS:]], 0)
    bc = lambda a: jnp.broadcast_to(a[:, :, None], (2 * RET_HEADS, c, LANE))
    lg2 = lg[:, None]
    wk = bc(jnp.concatenate([jnp.exp(lg2 * (c - 1.0 - pos)[None, :])[:RET_HEADS],
                             jnp.exp(lg2 * pos[None, :])[RET_HEADS:]], 0))
    wq = bc(jnp.concatenate([jnp.exp(lg2 * (pos + 1.0)[None, :])[:RET_HEADS],
                             jnp.exp(lg2 * (c - pos)[None, :])[RET_HEADS:]], 0))
    cd = jnp.broadcast_to(jnp.exp(lg * c)[:, None, None], (2 * RET_HEADS, 8, LANE))
    return dm, wk, wq, cd


def _retention(dims, p, cos_t, sin_t, tables):
    b, l, t, s, ct = dims["B"], dims["L"], dims["T"], dims["S"], dims["CT"]
    c = RET_CHUNK
    n = l // c
    n_ctx = ct // c
    hw = RET_HEADS * LANE
    dm, wk, wq, cd = tables
    fwd = lambda i: i
    bwd = lambda i: jnp.where(i < n_ctx, n_ctx - 1 - i, n - 1 + n_ctx - i)
    tab = lambda blkf: pl.BlockSpec((c, LANE), lambda bi, i: (jnp.where(blkf(i) < n_ctx, s // c, blkf(i) - n_ctx), 0))
    qkv = lambda blkf: [pl.BlockSpec((c, hw), functools.partial(lambda j, bi, i: (bi * n + blkf(i), j), j))
                        for j in (PB_RET_Q * LANE // hw, PB_RET_K * LANE // hw, PB_RET_V * LANE // hw)]
    full = lambda a: pl.BlockSpec(a.shape, lambda bi, i: (0, 0, 0))
    return pl.pallas_call(
        _retention_kernel,
        grid=(b, n),
        in_specs=qkv(fwd) + [tab(fwd), tab(fwd)] + qkv(bwd) + [tab(bwd), tab(bwd)]
        + [full(dm), full(wk), full(wq), full(cd)],
        out_specs=[pl.BlockSpec((c, hw), lambda bi, i: (bi * n + fwd(i), 0)),
                   pl.BlockSpec((c, hw), lambda bi, i: (bi * n + bwd(i), 0))],
        out_shape=[jax.ShapeDtypeStruct((t, hw), F32), jax.ShapeDtypeStruct((t, hw), F32)],
        scratch_shapes=[pltpu.VMEM((2 * RET_HEADS, RET_DK, LANE), F32)],
        compiler_params=_cparams("parallel", "arbitrary"), name="retention",
    )(p, p, p, cos_t, sin_t, p, p, p, cos_t, sin_t, dm, wk, wq, cd)


def _gdn_conv_kernel(ctx_len, tile, x_ref, w_ref, o_ref):
    l = x_ref.shape[0]
    cb = pl.program_id(1)
    w = w_ref[0]
    half = GDN_CONV // 2
    row = lax.broadcasted_iota(jnp.int32, (tile, LANE), 0)
    for r0 in range(0, l, tile):
        seg0, seg1 = (0, ctx_len) if r0 < ctx_len else (ctx_len, l)
        x = x_ref[r0:r0 + tile, :]
        acc = x * w[half:half + 1, :]
        for d in range(1, half + 1):
            if r0 - d >= seg0:
                back = x_ref[r0 - d:r0 - d + tile, :]
            else:
                back = jnp.where(row >= d, pltpu.roll(x, d, 0), 0.0)
            if r0 + tile + d <= seg1:
                fwd = x_ref[r0 + d:r0 + d + tile, :]
            else:
                fwd = jnp.where(row < tile - d, pltpu.roll(x, tile - d, 0), 0.0)
            acc = acc + back * w[half - d:half - d + 1, :] + fwd * w[half + d:half + d + 1, :]
        y = _silu(acc)
        nrm = lax.rsqrt(jnp.sum(y * y, -1, keepdims=True) + L2_EPS)
        fac = jnp.where(cb < 2 * GDN_HEADS, nrm, 1.0) * jnp.where(cb < GDN_HEADS, GDN_DK ** -0.5, 1.0)
        o_ref[r0:r0 + tile, :] = y * fac


def _gdn_conv(dims, p, conv_w):
    b, l, t, ct, rb = dims["B"], dims["L"], dims["T"], dims["CT"], dims["RB"]
    ncb = 3 * GDN_HEADS
    return pl.pallas_call(
        functools.partial(_gdn_conv_kernel, ct, rb),
        grid=(b, ncb),
        in_specs=[pl.BlockSpec((l, LANE), lambda bi, c: (bi, PB_GDN_Q + c)),
                  pl.BlockSpec((1, 8, LANE), lambda bi, c: (c, 0, 0))],
        out_specs=pl.BlockSpec((l, LANE), lambda bi, c: (bi, c)),
        out_shape=jax.ShapeDtypeStruct((t, ncb * LANE), F32),
        compiler_params=_cparams("parallel", "parallel"), name="gdn_conv",
    )(p, conv_w)


def _gdn_chain_kernel(qf_ref, kf_ref, vf_ref, abf_ref, qb_ref, kb_ref, vb_ref, abb_ref, gp_ref,
                      of_ref, ob_ref, s_ref, pa_ref, da_ref, pb_ref, db_ref):
    t = pl.program_id(1)

    @pl.when(t == 0)
    def _():
        s_ref[...] = jnp.zeros(s_ref.shape, F32)
        pb_ref[...] = jnp.zeros(pb_ref.shape, F32)
        db_ref[...] = jnp.zeros(db_ref.shape, F32)

    @pl.when(t % 2 == 0)
    def _():
        _gdn_chain_step(qf_ref, kf_ref, vf_ref, abf_ref, qb_ref, kb_ref, vb_ref, abb_ref, gp_ref,
                        of_ref, ob_ref, s_ref, pa_ref, da_ref, pb_ref, db_ref)

    @pl.when(t % 2 == 1)
    def _():
        _gdn_chain_step(qf_ref, kf_ref, vf_ref, abf_ref, qb_ref, kb_ref, vb_ref, abb_ref, gp_ref,
                        of_ref, ob_ref, s_ref, pb_ref, db_ref, pa_ref, da_ref)


_GDN_U, _GDN_W, _GDN_ATTN, _GDN_QIN, _GDN_KTAIL = range(5)


def _gdn_chain_step(qf_ref, kf_ref, vf_ref, abf_ref, qb_ref, kb_ref, vb_ref, abb_ref, gp_ref,
                    of_ref, ob_ref, s_ref, pw_ref, dw_ref, pr_ref, dr_ref):
    blk = GDN_BLOCK
    ri = lax.broadcasted_iota(jnp.int32, (blk, blk), 0)
    ci = lax.broadcasted_iota(jnp.int32, (blk, blk), 1)
    same = (ri // GDN_CHUNK) == (ci // GDN_CHUNK)
    sub = [(ri // m) == (ci // m) for m in (GDN_INV_BASE << i for i in range(8)) if m <= GDN_CHUNK]
    eye = jnp.where(ri == ci, 1.0, 0.0)
    lane = lax.broadcasted_iota(jnp.int32, (blk, LANE), 1)
    row = lax.broadcasted_iota(jnp.int32, (blk, LANE), 0)
    row_lo = row < GDN_CHUNK
    pos = row % GDN_CHUNK
    gp = gp_ref[...]

    def gates(ab):
        sp = ab + gp[1:2, :]
        la = -jnp.exp(gp[0:1, :]) * (jnp.maximum(sp, 0.0) + jnp.log1p(jnp.exp(-jnp.abs(sp))))
        return la, jax.nn.sigmoid(ab)

    la_f, beta_f = gates(abf_ref[...])
    la_b, beta_b = gates(abb_ref[...])
    g_f = la_f
    g_b = la_b
    sh = 1
    while sh < GDN_CHUNK:
        g_f = g_f + jnp.where(pos >= sh, pltpu.roll(g_f, sh, 0), 0.0)
        g_b = g_b + jnp.where(pos < GDN_CHUNK - sh, pltpu.roll(g_b, blk - sh, 0), 0.0)
        sh *= 2

    def pick(x, c):
        return jnp.broadcast_to(jnp.sum(jnp.where(lane == c, x, 0.0), -1, keepdims=True), (blk, LANE))

    n_chain = 2 * GDN_HEADS
    is_bwd = lambda c: c >= GDN_HEADS
    ends = lambda c: (0, GDN_CHUNK) if is_bwd(c) else (GDN_CHUNK - 1, blk - 1)

    def prepare():
        chains = []
        for h in range(GDN_HEADS):
            cols = slice(h * LANE, (h + 1) * LANE)
            for d, (q_ref, k_ref, v_ref, g, beta) in enumerate(
                    ((qf_ref, kf_ref, vf_ref, g_f, beta_f), (qb_ref, kb_ref, vb_ref, g_b, beta_b))):
                c = d * GDN_HEADS + h
                chains.append(dict(q=q_ref[:, cols], k=k_ref[:, cols], v=v_ref[:, cols], gb=pick(g, c),
                                   beta=pick(beta, 2 * GDN_HEADS + c), c=c))
        for ch in chains:
            gb = ch["gb"]
            gdiff = gb - gb.T
            incl = same & ((ci >= ri) if is_bwd(ch["c"]) else (ci <= ri))
            ch["strict"] = same & ((ci > ri) if is_bwd(ch["c"]) else (ci < ri))
            ch["decay"] = jnp.where(incl, jnp.exp(jnp.where(incl, gdiff, 0.0)), 0.0)
            ch["kb"] = ch["k"] * ch["beta"]
        yield
        for ch in chains:
            sc = _mm_nt(jnp.concatenate([ch["kb"], ch["q"]], 0), ch["k"])
            ch["a"] = jnp.where(ch["strict"], sc[:blk, :] * ch["decay"], 0.0)
            pw_ref[ch["c"], _GDN_ATTN] = sc[blk:, :] * ch["decay"]
            ch["p"] = -jnp.where(sub[0], ch["a"], 0.0)
            ch["t"] = eye + ch["p"]
        yield
        for ch in chains:
            ch["p"] = _mm(ch["p"], ch["p"])
        yield
        for _ in range(int(math.log2(GDN_INV_BASE)) - 2):
            for ch in chains:
                r = _mm(jnp.concatenate([ch["p"], ch["t"]], 0), ch["p"])
                ch["t"] = ch["t"] + r[blk:, :]
                ch["p"] = r[:blk, :]
            yield
        for ch in chains:
            ch["t"] = ch["t"] + _mm(ch["t"], ch["p"])
        yield
        for lvl in range(1, len(sub)):
            for ch in chains:
                a_off = jnp.where(sub[lvl] & jnp.logical_not(sub[lvl - 1]), ch["a"], 0.0)
                ch["x"] = _mm(a_off, ch["t"])
            yield
            for ch in chains:
                ch["t"] = ch["t"] - _mm(ch["t"], ch["x"])
            yield
        for ch in chains:
            c, gb = ch["c"], ch["gb"]
            eg = jnp.exp(gb)
            uw = _mm(ch["t"], jnp.concatenate([ch["v"] * ch["beta"], ch["kb"] * eg], -1))
            pw_ref[c, _GDN_U] = uw[:, :LANE]
            pw_ref[c, _GDN_W] = uw[:, LANE:]
            pw_ref[c, _GDN_QIN] = ch["q"] * eg
            e0, e1 = ends(c)
            g_end = jnp.where(row_lo, gb[e0:e0 + 1, :], gb[e1:e1 + 1, :])
            pw_ref[c, _GDN_KTAIL] = ch["k"] * jnp.exp(g_end - gb)
            dw_ref[c] = jnp.where(row[:8, :] == 0, jnp.exp(gb[e0:e0 + 1, :]), jnp.exp(gb[e1:e1 + 1, :]))
        yield

    def recur():
        state = [s_ref[c] for c in range(n_chain)]
        outs = [[None, None] for _ in range(n_chain)]
        vn = [None] * n_chain
        for step in range(2):
            chunk = lambda c: (1 - step) if is_bwd(c) else step
            rows = lambda c: slice(chunk(c) * GDN_CHUNK, (chunk(c) + 1) * GDN_CHUNK)
            for c in range(n_chain):
                v_new = pr_ref[c, _GDN_U, rows(c), :] - _mm(pr_ref[c, _GDN_W, rows(c), :], state[c])
                zero = jnp.zeros_like(v_new)
                vn[c] = jnp.concatenate([v_new, zero] if chunk(c) == 0 else [zero, v_new], 0)
            yield
            for c in range(n_chain):
                outs[c][chunk(c)] = _mm(
                    jnp.concatenate([pr_ref[c, _GDN_ATTN, rows(c), :], pr_ref[c, _GDN_QIN, rows(c), :]], -1),
                    jnp.concatenate([vn[c], state[c]], 0))
            yield
            for c in range(n_chain):
                in_c = row_lo if chunk(c) == 0 else jnp.logical_not(row_lo)
                dec = dr_ref[c][chunk(c):chunk(c) + 1, :]
                state[c] = state[c] * dec + _mm_tn(jnp.where(in_c, pr_ref[c, _GDN_KTAIL], 0.0), vn[c])
            yield
        for c in range(n_chain):
            o_ref = ob_ref if is_bwd(c) else of_ref
            h = c % GDN_HEADS
            o_ref[:, h * LANE:(h + 1) * LANE] = jnp.concatenate(outs[c], 0)
            s_ref[c] = state[c]
        yield

    live = [prepare(), recur()]
    while live:
        for gen in list(live):
            if next(gen, "done") == "done":
                live.remove(gen)


def _gdn_chain(dims, p, qkv, gate_params):
    b, l, t, ct = dims["B"], dims["L"], dims["T"], dims["CT"]
    blk = GDN_BLOCK
    n = l // blk
    n_ctx = ct // blk
    hw = GDN_HEADS * LANE
    fwd = lambda bi, i: bi * n + i
    bwd = lambda bi, i: bi * n + jnp.where(i < n_ctx, n_ctx - 1 - i, n - 1 + n_ctx - i)
    prep = lambda rowf: (lambda bi, i: rowf(bi, jnp.minimum(i, n - 1)))
    done = lambda rowf: (lambda bi, i: rowf(bi, jnp.maximum(i - 1, 0)))
    qkv_specs = lambda rowf: [pl.BlockSpec((blk, hw), functools.partial(lambda j, bi, i: (rowf(bi, i), j), j))
                              for j in range(3)]
    ab_spec = lambda rowf: pl.BlockSpec((blk, LANE), lambda bi, i: (rowf(bi, i), PB_GDN_AB))
    prepared = pltpu.VMEM((2 * GDN_HEADS, 5, blk, LANE), F32)
    decays = pltpu.VMEM((2 * GDN_HEADS, 8, LANE), F32)
    return pl.pallas_call(
        _gdn_chain_kernel,
        grid=(b, n + 1),
        in_specs=qkv_specs(prep(fwd)) + [ab_spec(prep(fwd))] + qkv_specs(prep(bwd)) + [ab_spec(prep(bwd))]
        + [pl.BlockSpec((8, LANE), lambda bi, i: (0, 0))],
        out_specs=[pl.BlockSpec((blk, hw), lambda bi, i: (done(fwd)(bi, i), 0)),
                   pl.BlockSpec((blk, hw), lambda bi, i: (done(bwd)(bi, i), 0))],
        out_shape=[jax.ShapeDtypeStruct((t, hw), F32), jax.ShapeDtypeStruct((t, hw), F32)],
        scratch_shapes=[pltpu.VMEM((2 * GDN_HEADS, GDN_DK, LANE), F32), prepared, decays, prepared, decays],
        compiler_params=_cparams("parallel", "arbitrary"), name="gdn_chain",
    )(qkv, qkv, qkv, p, qkv, qkv, qkv, p, gate_params)


def _outproj_kernel(n_experts, om_ref, rf_ref, rb_ref, rg_ref, gf_ref, gb_ref, z_ref, x_ref, gm_ref, gr_ref,
                    gg_ref, w_ref, gt_ref, g2_ref, sc_ref, sh_ref, wr_ref, br_ref, xo_ref, hp_ref, ti_ref, tw_ref):
    ret, gdn = [], []
    for h in range(RET_HEADS):
        cols = slice(h * LANE, (h + 1) * LANE)
        o = rf_ref[:, cols] + rb_ref[:, cols]
        oc = o - jnp.mean(o, -1, keepdims=True)
        var = jnp.mean(oc * oc, -1, keepdims=True)
        ret.append(oc * lax.rsqrt(var + GROUP_NORM_EPS) * gr_ref[:, cols] * _silu(rg_ref[:, cols]))
    for h in range(GDN_HEADS):
        cols = slice(h * LANE, (h + 1) * LANE)
        o = gf_ref[:, cols] + gb_ref[:, cols]
        gdn.append(_rms(o, gg_ref[...]) * _silu(z_ref[:, cols]))
    mixed = jnp.concatenate([_rms(om_ref[...], gm_ref[...])] + ret + gdn, -1)
    x = x_ref[...] + gt_ref[0] * _mm(mixed, w_ref[...])
    xo_ref[...] = x
    h = (_rms(x, g2_ref[...]) * (1.0 + sc_ref[0]) + sh_ref[0]).astype(MXU_DTYPE)
    hp_ref[...] = h.astype(F32)
    logits = _mm(h, wr_ref[...]) + br_ref[...]
    lane = lax.broadcasted_iota(jnp.int32, logits.shape, 1)
    lane_f = lane.astype(F32)
    neg = jnp.float32(-jnp.inf)
    logits = jnp.where(lane < n_experts, logits, neg)
    vals, idxs = [], []
    for _ in range(TOP_K):
        m = jnp.max(logits, -1, keepdims=True)
        idx = jnp.min(jnp.where(logits == m, lane_f, float(LANE)), -1, keepdims=True)
        logits = jnp.where(lane_f == idx, neg, logits)
        vals.append(m)
        idxs.append(idx.astype(jnp.int32))
    es = [jnp.exp(v - vals[0]) for v in vals]
    den = es[0]
    for e in es[1:]:
        den = den + e
    ti = jnp.zeros(logits.shape, jnp.int32)
    tw = jnp.zeros(logits.shape, F32)
    for k in range(TOP_K):
        ti = jnp.where(lane == k, idxs[k], ti)
        tw = jnp.where(lane == k, es[k] / den, tw)
    ti_ref[...] = ti
    tw_ref[...] = tw


def _outproj(dims, om, rf, rbk, gf, gb, p, x, mod_l, g_mla, g_ret, g_gdn, w_out, g2, w_router_p, b_router_p,
             n_experts):
    t, d, rb = dims["T"], dims["D"], dims["RB"]
    hw = GDN_HEADS * LANE
    row = lambda w: pl.BlockSpec((rb, w), lambda i: (i, 0))
    full = lambda a: pl.BlockSpec(a.shape, lambda i: (0, 0))
    pcol = lambda blk: pl.BlockSpec((rb, hw), lambda i: (i, blk * LANE // hw))
    return pl.pallas_call(
        functools.partial(_outproj_kernel, n_experts),
        grid=(t // rb,),
        in_specs=[row(om.shape[1]), row(hw), row(hw), pcol(PB_RET_G), row(hw), row(hw), pcol(PB_GDN_Z), row(d),
                  full(g_mla), full(g_ret), full(g_gdn), full(w_out),
                  _mod_spec(dims, 2), full(g2), _mod_spec(dims, 4), _mod_spec(dims, 3),
                  full(w_router_p), full(b_router_p)],
        out_specs=[row(d), row(d), row(LANE), row(LANE)],
        out_shape=[jax.ShapeDtypeStruct((t, d), F32), jax.ShapeDtypeStruct((t, d), F32),
                   jax.ShapeDtypeStruct((t, LANE), jnp.int32), jax.ShapeDtypeStruct((t, LANE), F32)],
        compiler_params=_cparams("parallel"), name="outproj_router",
    )(om, rf, rbk, p, gf, gb, p, x, g_mla, g_ret, g_gdn, w_out, mod_l, g2, mod_l, mod_l, w_router_p, b_router_p)


def _expert_kernel(be_ref, nu_ref, x_ref, wgu_ref, bgu_ref, wdn_ref, bdn_ref, y_ref, wgu_s, wdn_s):
    i = pl.program_id(0)
    f = wdn_ref.shape[2]
    prev = be_ref[jnp.maximum(i - 1, 0)]

    @pl.when(jnp.logical_or(i == 0, be_ref[i] != prev))
    def _():
        wgu_s[...] = wgu_ref[0, 0].astype(wgu_s.dtype)
        wdn_s[...] = wdn_ref[0, 0].astype(wdn_s.dtype)

    @pl.when(i < nu_ref[0])
    def _():
        gu = _mm(x_ref[...], wgu_s[...]) + bgu_ref[0]
        gate = jnp.minimum(gu[:, :f], SWIGLU_LIMIT)
        up = jnp.clip(gu[:, f:], -SWIGLU_LIMIT, SWIGLU_LIMIT)
        act = (up + 1.0) * (gate * jax.nn.sigmoid(gate * SWIGLU_ALPHA))
        y_ref[...] = _mm(act, wdn_s[...]) + bdn_ref[0]

    @pl.when(i >= nu_ref[0])
    def _():
        y_ref[...] = jnp.zeros(y_ref.shape, F32)


def _experts(li, xs, block_e, n_used, w_gu, b_gu, w_dn, b_dn):
    rows = xs.shape[0]
    depth, e, d, f2 = w_gu.shape
    f = f2 // 2
    nblk = rows // MOE_BLOCK
    grid_spec = pltpu.PrefetchScalarGridSpec(
        num_scalar_prefetch=2,
        grid=(nblk,),
        in_specs=[pl.BlockSpec((MOE_BLOCK, d), lambda i, be, nu: (i, 0)),
                  pl.BlockSpec((1, 1, d, f2), lambda i, be, nu: (li, be[i], 0, 0)),
                  pl.BlockSpec((1, 1, f2), lambda i, be, nu: (li * e + be[i], 0, 0)),
                  pl.BlockSpec((1, 1, f, d), lambda i, be, nu: (li, be[i], 0, 0)),
                  pl.BlockSpec((1, 1, d), lambda i, be, nu: (li * e + be[i], 0, 0))],
        out_specs=pl.BlockSpec((MOE_BLOCK, d), lambda i, be, nu: (i, 0)),
        scratch_shapes=[pltpu.VMEM((d, f2), MXU_DTYPE), pltpu.VMEM((f, d), MXU_DTYPE)],
    )
    return pl.pallas_call(
        _expert_kernel,
        grid_spec=grid_spec,
        out_shape=jax.ShapeDtypeStruct((rows, d), F32),
        compiler_params=_cparams("arbitrary"), name="experts",
    )(block_e, n_used, xs, w_gu, b_gu.reshape(depth * e, 1, f2), w_dn, b_dn.reshape(depth * e, 1, d))


def _row_token_kernel(d_ref, dt_ref, o_ref, acc_ref):
    i = pl.program_id(0)
    rb = d_ref.shape[0]
    nr = acc_ref.shape[0]

    @pl.when(i == 0)
    def _():
        acc_ref[...] = jnp.zeros(acc_ref.shape, F32)

    d = d_ref[...]
    dt = dt_ref[...]
    tok = i * rb + lax.broadcasted_iota(jnp.int32, (rb, 1), 0)
    hi = (tok >> LANE_BITS).astype(F32)
    lo = (tok & (LANE - 1)).astype(F32)
    sub_r = lax.broadcasted_iota(jnp.int32, (nr, rb), 0)
    lane_c = lax.broadcasted_iota(jnp.int32, (rb, 2 * LANE), 1)
    acc = acc_ref[...]
    for k in range(TOP_K):
        row_hot_t = jnp.where(sub_r == (dt[k:k + 1, :] >> LANE_BITS), 1.0, 0.0)
        col = d[:, k:k + 1] & (LANE - 1)
        col_val = jnp.where(lane_c == col, hi, jnp.where(lane_c == col + LANE, lo, 0.0))
        acc = acc + _mm(row_hot_t, col_val)
    acc_ref[...] = acc

    @pl.when(i == pl.num_programs(0) - 1)
    def _():
        o_ref[...] = (acc[:, :LANE] * float(LANE) + acc[:, LANE:]).astype(jnp.int32)


def _row_tokens(dest, rows, rb):
    t = dest.shape[0]
    assert rows % LANE == 0 and t % rb == 0
    nr = -(-(rows // LANE) // LANE) * LANE
    dest_p = jnp.pad(dest, ((0, 0), (0, LANE - TOP_K)))
    dest_t = jnp.pad(dest.T, ((0, 8 - TOP_K), (0, 0)))
    out = pl.pallas_call(
        _row_token_kernel,
        grid=(t // rb,),
        in_specs=[pl.BlockSpec((rb, LANE), lambda i: (i, 0)), pl.BlockSpec((8, rb), lambda i: (0, i))],
        out_specs=pl.BlockSpec((nr, LANE), lambda i: (0, 0)),
        out_shape=jax.ShapeDtypeStruct((nr, LANE), jnp.int32),
        scratch_shapes=[pltpu.VMEM((nr, 2 * LANE), F32)],
        compiler_params=_cparams("arbitrary"), name="row_tokens",
    )(dest_p, dest_t)
    return out.reshape(-1)[:rows]


def _moe(li, hp, top_i, w_gu, b_gu, w_dn, b_dn):
    t = hp.shape[0]
    e = w_gu.shape[1]
    idx = top_i[:, :TOP_K]
    onehot = idx[:, :, None] == jnp.arange(e, dtype=jnp.int32)[None, None, :]
    member = jnp.sum(onehot, 1).astype(jnp.int32)
    counts = jnp.sum(member, 0)
    rank = jnp.cumsum(member, 0) - member
    padded = (counts + MOE_BLOCK - 1) // MOE_BLOCK * MOE_BLOCK
    pend = jnp.cumsum(padded)
    pstart = pend - padded
    dest = jnp.sum(jnp.where(onehot, (rank + pstart[None, :])[:, None, :], 0), -1)
    nblk = -(-(t * TOP_K) // MOE_BLOCK) + e
    rows = nblk * MOE_BLOCK
    xs = hp[_row_tokens(dest, rows, math.gcd(ROW_BLOCK, t))]
    blk_row = jnp.arange(nblk, dtype=jnp.int32) * MOE_BLOCK
    block_e = jnp.minimum(jnp.sum(pend[None, :] <= blk_row[:, None], 1), e - 1).astype(jnp.int32)
    n_used = (pend[-1] // MOE_BLOCK).astype(jnp.int32).reshape(1)
    ys = _experts(li, xs, block_e, n_used, w_gu, b_gu, w_dn, b_dn)
    return ys[dest.T.reshape(-1)].reshape(TOP_K, t, ys.shape[1])


def _final_kernel(x_ref, tw_ref, *refs):
    y_refs = refs[:TOP_K]
    gt_ref, g_ref, o_ref = refs[TOP_K:]
    o_ref[...] = _rms(x_ref[...] + gt_ref[0] * _combine(tw_ref, y_refs), g_ref[...])


def _final(dims, x, moe, mod_l, g_final):
    b, s, d, rb, l, ct = dims["B"], dims["S"], dims["D"], dims["RB"], dims["L"], dims["CT"]
    ys, tw = moe
    ns = s // rb
    src = lambda i: (i // ns) * (l // rb) + ct // rb + i % ns
    return pl.pallas_call(
        _final_kernel,
        grid=(b * ns,),
        in_specs=[pl.BlockSpec((rb, d), lambda i: (src(i), 0)), pl.BlockSpec((rb, LANE), lambda i: (src(i), 0))]
        + _expert_row_specs(rb, d, src)
        + [pl.BlockSpec((1, 1, d), lambda i: (i // ns, 0, 5)), pl.BlockSpec((1, d), lambda i: (0, 0))],
        out_specs=pl.BlockSpec((rb, d), lambda i: (i, 0)),
        out_shape=jax.ShapeDtypeStruct((b * s, d), F32),
        compiler_params=_cparams("parallel"), name="final_norm",
    )(x, tw, *([ys] * TOP_K), mod_l, g_final)


def _rope_tables(s, rb, quarter, width):
    rows = (jnp.arange(s, dtype=jnp.int32) // GRID_W).astype(F32)
    cols = (jnp.arange(s, dtype=jnp.int32) % GRID_W).astype(F32)
    inv = ROPE_BASE ** (-jnp.arange(quarter, dtype=F32) / quarter)
    ar = rows[:, None] * inv[None, :]
    ac = cols[:, None] * inv[None, :]
    pad = width - 4 * quarter
    cos = jnp.concatenate([jnp.cos(ar), jnp.cos(ar), jnp.cos(ac), jnp.cos(ac), jnp.ones((s, pad), F32)], -1)
    sin = jnp.concatenate([-jnp.sin(ar), jnp.sin(ar), -jnp.sin(ac), jnp.sin(ac), jnp.zeros((s, pad), F32)], -1)
    cos = jnp.concatenate([cos, jnp.ones((rb, width), F32)], 0)
    sin = jnp.concatenate([sin, jnp.zeros((rb, width), F32)], 0)
    return cos, sin


def _layout_w_in(w):
    d = w.shape[0]
    n_mla = MLA_Q_LORA + MLA_KV_LORA + MLA_ROPE
    n_mid = 4 * RET_HEADS * RET_DK + 4 * GDN_HEADS * GDN_DK
    z = lambda n: jnp.zeros((d, n), w.dtype)
    out = jnp.concatenate([w[:, :n_mla], z(PB_RET_Q * LANE - n_mla), w[:, n_mla:n_mla + n_mid],
                           w[:, n_mla + n_mid:], z(LANE - 4 * GDN_HEADS)], -1)
    assert out.shape[1] == P_BLOCKS * LANE
    return out.astype(MXU_DTYPE)


def _layout_w_uq(w):
    r = w.shape[0]
    w = w.reshape(r, MLA_HEADS, MLA_NOPE + MLA_ROPE)
    w = jnp.concatenate([w, jnp.zeros((r, MLA_HEADS, MLA_QK - MLA_NOPE - MLA_ROPE), w.dtype)], -1)
    return w.reshape(r, MLA_HEADS * MLA_QK).astype(MXU_DTYPE)


def _layout_w_ukv(w):
    r = w.shape[0]
    w = w.reshape(r, MLA_HEADS, MLA_NOPE + MLA_V)
    wk = w[:, :, :MLA_NOPE].reshape(r, MLA_HEADS * MLA_NOPE)
    wv = w[:, :, MLA_NOPE:].reshape(r, MLA_HEADS * MLA_V)
    return wk.astype(MXU_DTYPE), wv.astype(MXU_DTYPE)


def kernel(x, c, ctx, c_ctx, w_mod, b_mod, g_norm1, g_norm2, w_in, g_cq, g_ckv, w_uq, w_ukv, g_mla, ret_decay, g_ret, gdn_conv, gdn_a_log, gdn_dt_bias, g_gdn, w_out, w_router, b_router, w_gate_up, b_gate_up, w_down, b_down, g_final):
    b, s, d = x.shape
    ct = ctx.shape[1]
    depth = w_mod.shape[0]
    n_experts = w_router.shape[-1]
    rb = math.gcd(ROW_BLOCK, ct)
    l = ct + s
    dims = dict(B=b, S=s, D=d, CT=ct, L=l, T=b * l, RB=rb)
    assert b + 1 <= 8 and ct % GDN_BLOCK == 0 and s % rb == 0 and n_experts <= LANE

    stream = jnp.concatenate([ctx, x], 1).reshape(b * l, d)
    cs = jnp.concatenate([c, c_ctx[None, :], jnp.zeros((7 - b, d), F32)], 0)
    mod = _modulation(cs, w_mod, b_mod).reshape(depth, 8, 1, 6 * d)

    cos_m, sin_m = _rope_tables(s, rb, MLA_ROPE // 4, LANE)
    cos_r, sin_r = _rope_tables(s, RET_CHUNK, RET_DK // 4, LANE)
    moe = None
    mod_prev = None
    for li in range(depth):
        w_in_p = _layout_w_in(w_in[li])
        stream, p = _inproj(dims, stream, moe, mod[li], mod_prev, g_norm1[li][None, :], w_in_p)
        wk, wv = _layout_w_ukv(w_ukv[li])
        q_m, k_m, v_m = _mla_prep(dims, p, g_cq[li][None, :], g_ckv[li][None, :], _layout_w_uq(w_uq[li]),
                                  wk, wv, cos_m, sin_m)
        o_mla = _attention(dims, q_m, k_m, v_m)
        o_rf, o_rb = _retention(dims, p, cos_r, sin_r, _retention_tables(ret_decay[li]))
        conv_w = jnp.zeros((8, 3 * GDN_HEADS * LANE), F32).at[:GDN_CONV].set(gdn_conv[li])
        conv_w = conv_w.reshape(8, 3 * GDN_HEADS, LANE).transpose(1, 0, 2)
        gate_params = jnp.zeros((8, LANE), F32)
        gate_params = gate_params.at[0, :2 * GDN_HEADS].set(gdn_a_log[li].reshape(-1))
        gate_params = gate_params.at[1, :2 * GDN_HEADS].set(gdn_dt_bias[li].reshape(-1))
        o_gf, o_gb = _gdn_chain(dims, p, _gdn_conv(dims, p, conv_w), gate_params)
        w_router_p = jnp.zeros((d, LANE), F32).at[:, :n_experts].set(w_router[li]).astype(MXU_DTYPE)
        b_router_p = jnp.zeros((1, LANE), F32).at[0, :n_experts].set(b_router[li])
        stream, hp, top_i, top_w = _outproj(dims, o_mla, o_rf, o_rb, o_gf, o_gb, p, stream, mod[li],
                                            g_mla[li][None, :], g_ret[li][None, :], g_gdn[li][None, :],
                                            w_out[li].astype(MXU_DTYPE),
                                            g_norm2[li][None, :], w_router_p, b_router_p, n_experts)
        moe = (_moe(li, hp, top_i, w_gate_up, b_gate_up, w_down, b_down), top_w)
        mod_prev = mod[li]
    return _final(dims, stream, moe, mod_prev, g_final[None, :]).reshape(b, s, d)
```

```python
import functools
import math

import jax
import jax.numpy as jnp
from jax import lax
from jax.experimental import pallas as pl
from jax.experimental.pallas import tpu as pltpu

F32 = jnp.float32
MXU_DTYPE = jnp.bfloat16

LANE = 128
LANE_BITS = LANE.bit_length() - 1
V7X_VMEM_LIMIT_BYTES = 56 * 1024 * 1024

MLA_HEADS = 4
MLA_NOPE = 128
MLA_ROPE = 64
MLA_V = 128
MLA_Q_LORA = 256
MLA_KV_LORA = 128
MLA_QK = 256
MLA_VW = 256
RET_HEADS = 4
RET_DK = 128
GDN_HEADS = 4
GDN_DK = 128
GDN_CONV = 5
GDN_CHUNK = 64
GDN_BLOCK = 2 * GDN_CHUNK
GDN_INV_BASE = 16
RET_CHUNK = 256
TOP_K = 4
SWIGLU_LIMIT = 7.0
SWIGLU_ALPHA = 1.702
MOE_BLOCK = 256
GRID_W = 64
ROPE_BASE = 10000.0
NORM_EPS = 1e-6
GROUP_NORM_EPS = 1e-5
L2_EPS = 1e-6
ROW_BLOCK = 256

PB_DQ = 0
PB_DKV = 2
PB_RET_Q, PB_RET_K, PB_RET_V, PB_RET_G = 4, 8, 12, 16
PB_GDN_Q, PB_GDN_K, PB_GDN_V, PB_GDN_Z = 20, 24, 28, 32
PB_GDN_AB = 36
P_BLOCKS = 37


def _cparams(*sem):
    return pltpu.CompilerParams(dimension_semantics=sem, vmem_limit_bytes=V7X_VMEM_LIMIT_BYTES)


def _mm(a, b):
    return jnp.dot(a.astype(MXU_DTYPE), b.astype(MXU_DTYPE), preferred_element_type=F32)


def _mm_nt(a, b):
    return lax.dot_general(a.astype(MXU_DTYPE), b.astype(MXU_DTYPE), (((1,), (1,)), ((), ())),
                           preferred_element_type=F32)


def _mm_tn(a, b):
    return jnp.dot(a.T.astype(MXU_DTYPE), b.astype(MXU_DTYPE), preferred_element_type=F32)


def _rms(x, g, eps=NORM_EPS):
    return x * lax.rsqrt(jnp.mean(x * x, -1, keepdims=True) + eps) * g


def _silu(x):
    return x * jax.nn.sigmoid(x)


def _swap_pairs(x, q):
    n = x.shape[-1]
    lane = lax.broadcasted_iota(jnp.int32, x.shape, x.ndim - 1)
    first = (lane % (2 * q)) < q
    return jnp.where(first, pltpu.roll(x, n - q, x.ndim - 1), pltpu.roll(x, q, x.ndim - 1))


def _mod_kernel(c_ref, w_ref, b_ref, o_ref):
    o_ref[0] = _mm(_silu(c_ref[...]), w_ref[0]) + b_ref[0]


def _modulation(cs, w_mod, b_mod):
    depth, d, n = w_mod.shape
    tn = n // 6
    return pl.pallas_call(
        _mod_kernel,
        grid=(depth, n // tn),
        in_specs=[pl.BlockSpec((8, d), lambda l, j: (0, 0)),
                  pl.BlockSpec((1, d, tn), lambda l, j: (l, 0, j)),
                  pl.BlockSpec((1, 1, tn), lambda l, j: (l, 0, j))],
        out_specs=pl.BlockSpec((1, 8, tn), lambda l, j: (l, 0, j)),
        out_shape=jax.ShapeDtypeStruct((depth, 8, n), F32),
        compiler_params=_cparams("parallel", "parallel"),
        name="modulation",
    )(cs, w_mod, b_mod.reshape(depth, 1, n))


def _combine(tw_ref, y_refs):
    tw = tw_ref[...]
    y = y_refs[0][0] * tw[:, 0:1]
    for k in range(1, TOP_K):
        y = y + y_refs[k][0] * tw[:, k:k + 1]
    return y


def _inproj_kernel(has_y, *refs):
    if has_y:
        x_ref, tw_ref = refs[0], refs[1]
        y_refs = refs[2:2 + TOP_K]
        gt_ref, g_ref, sc_ref, sh_ref, w_ref, xo_ref, p_ref = refs[2 + TOP_K:]
        x = x_ref[...] + gt_ref[0] * _combine(tw_ref, y_refs)
        xo_ref[...] = x
    else:
        x_ref, g_ref, sc_ref, sh_ref, w_ref, p_ref = refs
        x = x_ref[...]
    h = _rms(x, g_ref[...]) * (1.0 + sc_ref[0]) + sh_ref[0]
    p_ref[...] = _mm(h, w_ref[...])


def _mod_spec(dims, chunk):
    d = dims["D"]
    return pl.BlockSpec((1, 1, d), lambda i: (_mod_index(dims, i), 0, chunk))


def _mod_index(dims, i):
    nb = dims["L"] // dims["RB"]
    return jnp.where(i % nb < dims["CT"] // dims["RB"], dims["B"], i // nb)


def _expert_row_specs(rb, d, row_block):
    return [pl.BlockSpec((1, rb, d), functools.partial(lambda k, i: (k, row_block(i), 0), k)) for k in range(TOP_K)]


def _inproj(dims, x, moe, mod_l, mod_prev, g1, w_in_p):
    t, d, rb = dims["T"], dims["D"], dims["RB"]
    np_ = w_in_p.shape[1]
    row = pl.BlockSpec((rb, d), lambda i: (i, 0))
    vec = pl.BlockSpec((1, d), lambda i: (0, 0))
    wspec = pl.BlockSpec((d, np_), lambda i: (0, 0))
    pspec = pl.BlockSpec((rb, np_), lambda i: (i, 0))
    pshape = jax.ShapeDtypeStruct((t, np_), F32)
    if moe is None:
        return x, pl.pallas_call(
            functools.partial(_inproj_kernel, False),
            grid=(t // rb,),
            in_specs=[row, vec, _mod_spec(dims, 1), _mod_spec(dims, 0), wspec],
            out_specs=pspec, out_shape=pshape,
            compiler_params=_cparams("parallel"), name="inproj",
        )(x, g1, mod_l, mod_l, w_in_p)
    ys, tw = moe
    return pl.pallas_call(
        functools.partial(_inproj_kernel, True),
        grid=(t // rb,),
        in_specs=[row, pl.BlockSpec((rb, LANE), lambda i: (i, 0))] + _expert_row_specs(rb, d, lambda i: i)
        + [_mod_spec(dims, 5), vec, _mod_spec(dims, 1), _mod_spec(dims, 0), wspec],
        out_specs=[row, pspec],
        out_shape=[jax.ShapeDtypeStruct((t, d), F32), pshape],
        compiler_params=_cparams("parallel"), name="inproj_res",
    )(x, tw, *([ys] * TOP_K), mod_prev, g1, mod_l, mod_l, w_in_p)


def _mla_prep_kernel(dq_ref, x2_ref, gq_ref, gkv_ref, wq_ref, wk_ref, wv_ref, cos_ref, sin_ref,
                     q_ref, k_ref, v_ref):
    cos = cos_ref[...]
    sin = sin_ref[...]

    def rope(r):
        return r * cos + _swap_pairs(r, MLA_ROPE // 4) * sin

    q = _mm(_rms(dq_ref[...], gq_ref[...]), wq_ref[...])
    x2 = x2_ref[...]
    kvn = _rms(x2[:, :MLA_KV_LORA], gkv_ref[...])
    kn = _mm(kvn, wk_ref[...])
    v = _mm(kvn, wv_ref[...])
    kr = rope(x2[:, MLA_KV_LORA:]).astype(k_ref.dtype)
    ones = jnp.ones((v.shape[0], MLA_VW - MLA_V), v_ref.dtype)
    for h in range(MLA_HEADS):
        o = h * MLA_QK
        q_ref[:, o:o + MLA_NOPE] = q[:, o:o + MLA_NOPE].astype(q_ref.dtype)
        q_ref[:, o + MLA_NOPE:o + MLA_QK] = rope(q[:, o + MLA_NOPE:o + MLA_QK]).astype(q_ref.dtype)
        k_ref[:, o:o + MLA_NOPE] = kn[:, h * MLA_NOPE:(h + 1) * MLA_NOPE].astype(k_ref.dtype)
        k_ref[:, o + MLA_NOPE:o + MLA_QK] = kr
        v_ref[:, h * MLA_VW:h * MLA_VW + MLA_V] = v[:, h * MLA_V:(h + 1) * MLA_V].astype(v_ref.dtype)
        v_ref[:, h * MLA_VW + MLA_V:(h + 1) * MLA_VW] = ones


def _pos_block(dims, i):
    nb = dims["L"] // dims["RB"]
    nc = dims["CT"] // dims["RB"]
    w = i % nb
    return jnp.where(w < nc, dims["S"] // dims["RB"], w - nc)


def _mla_prep(dims, p, g_cq, g_ckv, wq, wk, wv, cos_t, sin_t):
    t, rb = dims["T"], dims["RB"]
    hq = MLA_HEADS * MLA_QK
    hv = MLA_HEADS * MLA_VW
    full = lambda a: pl.BlockSpec(a.shape, lambda i: (0, 0))
    tab = pl.BlockSpec((rb, LANE), lambda i: (_pos_block(dims, i), 0))
    return pl.pallas_call(
        _mla_prep_kernel,
        grid=(t // rb,),
        in_specs=[pl.BlockSpec((rb, 2 * LANE), lambda i: (i, 0)),
                  pl.BlockSpec((rb, 2 * LANE), lambda i: (i, 1)),
                  full(g_cq), full(g_ckv), full(wq), full(wk), full(wv), tab, tab],
        out_specs=[pl.BlockSpec((rb, hq), lambda i: (i, 0)),
                   pl.BlockSpec((rb, hq), lambda i: (i, 0)),
                   pl.BlockSpec((rb, hv), lambda i: (i, 0))],
        out_shape=[jax.ShapeDtypeStruct((t, hq), MXU_DTYPE),
                   jax.ShapeDtypeStruct((t, hq), MXU_DTYPE),
                   jax.ShapeDtypeStruct((t, hv), MXU_DTYPE)],
        compiler_params=_cparams("parallel"), name="mla_prep",
    )(p, p, g_cq, g_ckv, wq, wk, wv, cos_t, sin_t)


_ATTN_EXP2_SCALE = (MLA_NOPE + MLA_ROPE) ** -0.5 * math.log2(math.e)


def _softmax_pv(s, v):
    m = jnp.max(s, -1, keepdims=True)
    e = jnp.exp2(s * _ATTN_EXP2_SCALE - m * _ATTN_EXP2_SCALE)
    pv = _mm(e, v)
    return pv[:, :MLA_V] / pv[:, MLA_V:MLA_V + 1]


def _attn_kernel(n_ctx_blocks, ctx_len, q_ref, k_ref, v_ref, o_ref, sa_ref, sb_ref):
    j = pl.program_id(2)
    n_keys = k_ref.shape[0]

    def step(write_ref, read_ref, keys):
        write_ref[...] = _mm_nt(q_ref[...], k_ref[...])
        if keys:
            o_ref[...] = _softmax_pv(read_ref[:, 0:keys], v_ref[0:keys, :])

    for parity, (write_ref, read_ref) in enumerate(((sa_ref, sb_ref), (sb_ref, sa_ref))):
        @pl.when(jnp.logical_and(j % 2 == parity, j == 0))
        def _():
            step(write_ref, read_ref, 0)

        @pl.when(jnp.logical_and(j % 2 == parity, jnp.logical_and(j >= 1, j <= n_ctx_blocks)))
        def _():
            step(write_ref, read_ref, ctx_len)

        @pl.when(jnp.logical_and(j % 2 == parity, j > n_ctx_blocks))
        def _():
            step(write_ref, read_ref, n_keys)


def _attention(dims, q, k, v):
    b, l, t, rb, ct = dims["B"], dims["L"], dims["T"], dims["RB"], dims["CT"]
    nq = l // rb
    q_blk = lambda bi, j: bi * nq + jnp.minimum(j, nq - 1)
    o_blk = lambda bi, j: bi * nq + jnp.maximum(j - 1, 0)
    return pl.pallas_call(
        functools.partial(_attn_kernel, ct // rb, ct),
        grid=(b, MLA_HEADS, nq + 1),
        in_specs=[pl.BlockSpec((rb, MLA_QK), lambda bi, h, j: (q_blk(bi, j), h)),
                  pl.BlockSpec((l, MLA_QK), lambda bi, h, j: (bi, h)),
                  pl.BlockSpec((l, MLA_VW), lambda bi, h, j: (bi, h))],
        out_specs=pl.BlockSpec((rb, MLA_V), lambda bi, h, j: (o_blk(bi, j), h)),
        out_shape=jax.ShapeDtypeStruct((t, MLA_HEADS * MLA_V), F32),
        scratch_shapes=[pltpu.VMEM((rb, l), F32), pltpu.VMEM((rb, l), F32)],
        compiler_params=_cparams("parallel", "parallel", "arbitrary"), name="mla_attention",
    )(q, k, v)


def _retention_kernel(qf_ref, kf_ref, vf_ref, cosf_ref, sinf_ref, qb_ref, kb_ref, vb_ref, cosb_ref, sinb_ref,
                      dm_ref, wk_ref, wq_ref, cd_ref, of_ref, ob_ref, s_ref):
    @pl.when(pl.program_id(1) == 0)
    def _():
        s_ref[...] = jnp.zeros(s_ref.shape, F32)

    scale = RET_DK ** -0.5
    chains = []
    for h in range(RET_HEADS):
        cols = slice(h * LANE, (h + 1) * LANE)
        for d, (q_ref, k_ref, v_ref, cos_ref, sin_ref, o_ref) in enumerate(
                ((qf_ref, kf_ref, vf_ref, cosf_ref, sinf_ref, of_ref),
                 (qb_ref, kb_ref, vb_ref, cosb_ref, sinb_ref, ob_ref))):
            rope = lambda x: x * cos_ref[...] + _swap_pairs(x, RET_DK // 4) * sin_ref[...]
            c = d * RET_HEADS + h
            chains.append(dict(q=rope(q_ref[:, cols]), k=rope(k_ref[:, cols]) * scale, v=v_ref[:, cols],
                               s=s_ref[c], c=c, o_ref=o_ref, cols=cols))
    for ch in chains:
        ch["sc"] = _mm_nt(ch["q"], ch["k"]) * dm_ref[ch["c"]]
    for ch in chains:
        ch["o"] = _mm(jnp.concatenate([ch["sc"], ch["q"] * wq_ref[ch["c"]]], -1),
                      jnp.concatenate([ch["v"], ch["s"]], 0))
    for ch in chains:
        c = ch["c"]
        ch["s"] = cd_ref[c][0:1, :] * ch["s"] + _mm_tn(ch["k"] * wk_ref[c], ch["v"])
    for ch in chains:
        ch["o_ref"][:, ch["cols"]] = ch["o"]
        s_ref[ch["c"]] = ch["s"]


def _retention_tables(ret_decay):
    c = RET_CHUNK
    lg = jnp.log1p(-jnp.exp(ret_decay.astype(F32))).reshape(2 * RET_HEADS)
    pos = jnp.arange(c, dtype=F32)
    diff = pos[:, None] - pos[None, :]
    lgm = lg[:, None, None]
    dm_f = jnp.where(diff >= 0, jnp.exp(lgm * jnp.maximum(diff, 0.0)), 0.0)
    dm_b = jnp.where(diff <= 0, jnp.exp(lgm * jnp.maximum(-diff, 0.0)), 0.0)
    dm = jnp.concatenate([dm_f[:RET_HEADS], dm_b[RET_HEADS:]], 0)
    bc = lambda a: jnp.broadcast_to(a[:, :, None], (2 * RET_HEADS, c, LANE))
    lg2 = lg[:, None]
    wk = bc(jnp.concatenate([jnp.exp(lg2 * (c - 1.0 - pos)[None, :])[:RET_HEADS],
                             jnp.exp(lg2 * pos[None, :])[RET_HEADS:]], 0))
    wq = bc(jnp.concatenate([jnp.exp(lg2 * (pos + 1.0)[None, :])[:RET_HEADS],
                             jnp.exp(lg2 * (c - pos)[None, :])[RET_HEADS:]], 0))
    cd = jnp.broadcast_to(jnp.exp(lg * c)[:, None, None], (2 * RET_HEADS, 8, LANE))
    return dm, wk, wq, cd


def _retention(dims, p, cos_t, sin_t, tables):
    b, l, t, s, ct = dims["B"], dims["L"], dims["T"], dims["S"], dims["CT"]
    c = RET_CHUNK
    n = l // c
    n_ctx = ct // c
    hw = RET_HEADS * LANE
    dm, wk, wq, cd = tables
    fwd = lambda i: i
    bwd = lambda i: jnp.where(i < n_ctx, n_ctx - 1 - i, n - 1 + n_ctx - i)
    tab = lambda blkf: pl.BlockSpec((c, LANE), lambda bi, i: (jnp.where(blkf(i) < n_ctx, s // c, blkf(i) - n_ctx), 0))
    qkv = lambda blkf: [pl.BlockSpec((c, hw), functools.partial(lambda j, bi, i: (bi * n + blkf(i), j), j))
                        for j in (PB_RET_Q * LANE // hw, PB_RET_K * LANE // hw, PB_RET_V * LANE // hw)]
    full = lambda a: pl.BlockSpec(a.shape, lambda bi, i: (0, 0, 0))
    return pl.pallas_call(
        _retention_kernel,
        grid=(b, n),
        in_specs=qkv(fwd) + [tab(fwd), tab(fwd)] + qkv(bwd) + [tab(bwd), tab(bwd)]
        + [full(dm), full(wk), full(wq), full(cd)],
        out_specs=[pl.BlockSpec((c, hw), lambda bi, i: (bi * n + fwd(i), 0)),
                   pl.BlockSpec((c, hw), lambda bi, i: (bi * n + bwd(i), 0))],
        out_shape=[jax.ShapeDtypeStruct((t, hw), F32), jax.ShapeDtypeStruct((t, hw), F32)],
        scratch_shapes=[pltpu.VMEM((2 * RET_HEADS, RET_DK, LANE), F32)],
        compiler_params=_cparams("parallel", "arbitrary"), name="retention",
    )(p, p, p, cos_t, sin_t, p, p, p, cos_t, sin_t, dm, wk, wq, cd)


def _gdn_conv_kernel(ctx_len, tile, x_ref, w_ref, o_ref):
    l = x_ref.shape[0]
    cb = pl.program_id(1)
    w = w_ref[0]
    half = GDN_CONV // 2
    row = lax.broadcasted_iota(jnp.int32, (tile, LANE), 0)
    for r0 in range(0, l, tile):
        seg0, seg1 = (0, ctx_len) if r0 < ctx_len else (ctx_len, l)
        x = x_ref[r0:r0 + tile, :]
        acc = x * w[half:half + 1, :]
        for d in range(1, half + 1):
            if r0 - d >= seg0:
                back = x_ref[r0 - d:r0 - d + tile, :]
            else:
                back = jnp.where(row >= d, pltpu.roll(x, d, 0), 0.0)
            if r0 + tile + d <= seg1:
                fwd = x_ref[r0 + d:r0 + d + tile, :]
            else:
                fwd = jnp.where(row < tile - d, pltpu.roll(x, tile - d, 0), 0.0)
            acc = acc + back * w[half - d:half - d + 1, :] + fwd * w[half + d:half + d + 1, :]
        y = _silu(acc)
        nrm = lax.rsqrt(jnp.sum(y * y, -1, keepdims=True) + L2_EPS)
        fac = jnp.where(cb < 2 * GDN_HEADS, nrm, 1.0) * jnp.where(cb < GDN_HEADS, GDN_DK ** -0.5, 1.0)
        o_ref[r0:r0 + tile, :] = y * fac


def _gdn_conv(dims, p, conv_w):
    b, l, t, ct, rb = dims["B"], dims["L"], dims["T"], dims["CT"], dims["RB"]
    ncb = 3 * GDN_HEADS
    return pl.pallas_call(
        functools.partial(_gdn_conv_kernel, ct, rb),
        grid=(b, ncb),
        in_specs=[pl.BlockSpec((l, LANE), lambda bi, c: (bi, PB_GDN_Q + c)),
                  pl.BlockSpec((1, 8, LANE), lambda bi, c: (c, 0, 0))],
        out_specs=pl.BlockSpec((l, LANE), lambda bi, c: (bi, c)),
        out_shape=jax.ShapeDtypeStruct((t, ncb * LANE), F32),
        compiler_params=_cparams("parallel", "parallel"), name="gdn_conv",
    )(p, conv_w)


def _gdn_chain_kernel(qf_ref, kf_ref, vf_ref, abf_ref, qb_ref, kb_ref, vb_ref, abb_ref, gp_ref,
                      of_ref, ob_ref, s_ref, pa_ref, da_ref, pb_ref, db_ref):
    t = pl.program_id(1)

    @pl.when(t == 0)
    def _():
        s_ref[...] = jnp.zeros(s_ref.shape, F32)
        pb_ref[...] = jnp.zeros(pb_ref.shape, F32)
        db_ref[...] = jnp.zeros(db_ref.shape, F32)

    @pl.when(t % 2 == 0)
    def _():
        _gdn_chain_step(qf_ref, kf_ref, vf_ref, abf_ref, qb_ref, kb_ref, vb_ref, abb_ref, gp_ref,
                        of_ref, ob_ref, s_ref, pa_ref, da_ref, pb_ref, db_ref)

    @pl.when(t % 2 == 1)
    def _():
        _gdn_chain_step(qf_ref, kf_ref, vf_ref, abf_ref, qb_ref, kb_ref, vb_ref, abb_ref, gp_ref,
                        of_ref, ob_ref, s_ref, pb_ref, db_ref, pa_ref, da_ref)


_GDN_U, _GDN_W, _GDN_ATTN, _GDN_QIN, _GDN_KTAIL = range(5)


def _gdn_chain_step(qf_ref, kf_ref, vf_ref, abf_ref, qb_ref, kb_ref, vb_ref, abb_ref, gp_ref,
                    of_ref, ob_ref, s_ref, pw_ref, dw_ref, pr_ref, dr_ref):
    blk = GDN_BLOCK
    ri = lax.broadcasted_iota(jnp.int32, (blk, blk), 0)
    ci = lax.broadcasted_iota(jnp.int32, (blk, blk), 1)
    same = (ri // GDN_CHUNK) == (ci // GDN_CHUNK)
    sub = [(ri // m) == (ci // m) for m in (GDN_INV_BASE << i for i in range(8)) if m <= GDN_CHUNK]
    eye = jnp.where(ri == ci, 1.0, 0.0)
    lane = lax.broadcasted_iota(jnp.int32, (blk, LANE), 1)
    row = lax.broadcasted_iota(jnp.int32, (blk, LANE), 0)
    row_lo = row < GDN_CHUNK
    pos = row % GDN_CHUNK
    gp = gp_ref[...]

    def gates(ab):
        sp = ab + gp[1:2, :]
        la = -jnp.exp(gp[0:1, :]) * (jnp.maximum(sp, 0.0) + jnp.log1p(jnp.exp(-jnp.abs(sp))))
        return la, jax.nn.sigmoid(ab)

    la_f, beta_f = gates(abf_ref[...])
    la_b, beta_b = gates(abb_ref[...])
    g_f = la_f
    g_b = la_b
    sh = 1
    while sh < GDN_CHUNK:
        g_f = g_f + jnp.where(pos >= sh, pltpu.roll(g_f, sh, 0), 0.0)
        g_b = g_b + jnp.where(pos < GDN_CHUNK - sh, pltpu.roll(g_b, blk - sh, 0), 0.0)
        sh *= 2

    def pick(x, c):
        return jnp.broadcast_to(jnp.sum(jnp.where(lane == c, x, 0.0), -1, keepdims=True), (blk, LANE))

    n_chain = 2 * GDN_HEADS
    is_bwd = lambda c: c >= GDN_HEADS
    ends = lambda c: (0, GDN_CHUNK) if is_bwd(c) else (GDN_CHUNK - 1, blk - 1)

    def prepare():
        chains = []
        for h in range(GDN_HEADS):
            cols = slice(h * LANE, (h + 1) * LANE)
            for d, (q_ref, k_ref, v_ref, g, beta) in enumerate(
                    ((qf_ref, kf_ref, vf_ref, g_f, beta_f), (qb_ref, kb_ref, vb_ref, g_b, beta_b))):
                c = d * GDN_HEADS + h
                chains.append(dict(q=q_ref[:, cols], k=k_ref[:, cols], v=v_ref[:, cols], gb=pick(g, c),
                                   beta=pick(beta, 2 * GDN_HEADS + c), c=c))
        for ch in chains:
            gb = ch["gb"]
            gdiff = gb - gb.T
            incl = same & ((ci >= ri) if is_bwd(ch["c"]) else (ci <= ri))
            ch["strict"] = same & ((ci > ri) if is_bwd(ch["c"]) else (ci < ri))
            ch["decay"] = jnp.where(incl, jnp.exp(jnp.where(incl, gdiff, 0.0)), 0.0)
            ch["kb"] = ch["k"] * ch["beta"]
        yield
        for ch in chains:
            sc = _mm_nt(jnp.concatenate([ch["kb"], ch["q"]], 0), ch["k"])
            ch["a"] = jnp.where(ch["strict"], sc[:blk, :] * ch["decay"], 0.0)
            pw_ref[ch["c"], _GDN_ATTN] = sc[blk:, :] * ch["decay"]
            ch["p"] = -jnp.where(sub[0], ch["a"], 0.0)
            ch["t"] = eye + ch["p"]
        yield
        for ch in chains:
            ch["p"] = _mm(ch["p"], ch["p"])
        yield
        for _ in range(int(math.log2(GDN_INV_BASE)) - 2):
            for ch in chains:
                r = _mm(jnp.concatenate([ch["p"], ch["t"]], 0), ch["p"])
                ch["t"] = ch["t"] + r[blk:, :]
                ch["p"] = r[:blk, :]
            yield
        for ch in chains:
            ch["t"] = ch["t"] + _mm(ch["t"], ch["p"])
        yield
        for lvl in range(1, len(sub)):
            for ch in chains:
                a_off = jnp.where(sub[lvl] & jnp.logical_not(sub[lvl - 1]), ch["a"], 0.0)
                ch["x"] = _mm(a_off, ch["t"])
            yield
            for ch in chains:
                ch["t"] = ch["t"] - _mm(ch["t"], ch["x"])
            yield
        for ch in chains:
            c, gb = ch["c"], ch["gb"]
            eg = jnp.exp(gb)
            uw = _mm(ch["t"], jnp.concatenate([ch["v"] * ch["beta"], ch["kb"] * eg], -1))
            pw_ref[c, _GDN_U] = uw[:, :LANE]
            pw_ref[c, _GDN_W] = uw[:, LANE:]
            pw_ref[c, _GDN_QIN] = ch["q"] * eg
            e0, e1 = ends(c)
            g_end = jnp.where(row_lo, gb[e0:e0 + 1, :], gb[e1:e1 + 1, :])
            pw_ref[c, _GDN_KTAIL] = ch["k"] * jnp.exp(g_end - gb)
            dw_ref[c] = jnp.where(row[:8, :] == 0, jnp.exp(gb[e0:e0 + 1, :]), jnp.exp(gb[e1:e1 + 1, :]))
        yield

    def recur():
        state = [s_ref[c] for c in range(n_chain)]
        outs = [[None, None] for _ in range(n_chain)]
        vn = [None] * n_chain
        for step in range(2):
            chunk = lambda c: (1 - step) if is_bwd(c) else step
            rows = lambda c: slice(chunk(c) * GDN_CHUNK, (chunk(c) + 1) * GDN_CHUNK)
            for c in range(n_chain):
                v_new = pr_ref[c, _GDN_U, rows(c), :] - _mm(pr_ref[c, _GDN_W, rows(c), :], state[c])
                zero = jnp.zeros_like(v_new)
                vn[c] = jnp.concatenate([v_new, zero] if chunk(c) == 0 else [zero, v_new], 0)
            yield
            for c in range(n_chain):
                outs[c][chunk(c)] = _mm(
                    jnp.concatenate([pr_ref[c, _GDN_ATTN, rows(c), :], pr_ref[c, _GDN_QIN, rows(c), :]], -1),
                    jnp.concatenate([vn[c], state[c]], 0))
            yield
            for c in range(n_chain):
                in_c = row_lo if chunk(c) == 0 else jnp.logical_not(row_lo)
                dec = dr_ref[c][chunk(c):chunk(c) + 1, :]
                state[c] = state[c] * dec + _mm_tn(jnp.where(in_c, pr_ref[c, _GDN_KTAIL], 0.0), vn[c])
            yield
        for c in range(n_chain):
            o_ref = ob_ref if is_bwd(c) else of_ref
            h = c % GDN_HEADS
            o_ref[:, h * LANE:(h + 1) * LANE] = jnp.concatenate(outs[c], 0)
            s_ref[c] = state[c]
        yield

    live = [prepare(), recur()]
    while live:
        for gen in list(live):
            if next(gen, "done") == "done":
                live.remove(gen)


def _gdn_chain(dims, p, qkv, gate_params):
    b, l, t, ct = dims["B"], dims["L"], dims["T"], dims["CT"]
    blk = GDN_BLOCK
    n = l // blk
    n_ctx = ct // blk
    hw = GDN_HEADS * LANE
    fwd = lambda bi, i: bi * n + i
    bwd = lambda bi, i: bi * n + jnp.where(i < n_ctx, n_ctx - 1 - i, n - 1 + n_ctx - i)
    prep = lambda rowf: (lambda bi, i: rowf(bi, jnp.minimum(i, n - 1)))
    done = lambda rowf: (lambda bi, i: rowf(bi, jnp.maximum(i - 1, 0)))
    qkv_specs = lambda rowf: [pl.BlockSpec((blk, hw), functools.partial(lambda j, bi, i: (rowf(bi, i), j), j))
                              for j in range(3)]
    ab_spec = lambda rowf: pl.BlockSpec((blk, LANE), lambda bi, i: (rowf(bi, i), PB_GDN_AB))
    prepared = pltpu.VMEM((2 * GDN_HEADS, 5, blk, LANE), F32)
    decays = pltpu.VMEM((2 * GDN_HEADS, 8, LANE), F32)
    return pl.pallas_call(
        _gdn_chain_kernel,
        grid=(b, n + 1),
        in_specs=qkv_specs(prep(fwd)) + [ab_spec(prep(fwd))] + qkv_specs(prep(bwd)) + [ab_spec(prep(bwd))]
        + [pl.BlockSpec((8, LANE), lambda bi, i: (0, 0))],
        out_specs=[pl.BlockSpec((blk, hw), lambda bi, i: (done(fwd)(bi, i), 0)),
                   pl.BlockSpec((blk, hw), lambda bi, i: (done(bwd)(bi, i), 0))],
        out_shape=[jax.ShapeDtypeStruct((t, hw), F32), jax.ShapeDtypeStruct((t, hw), F32)],
        scratch_shapes=[pltpu.VMEM((2 * GDN_HEADS, GDN_DK, LANE), F32), prepared, decays, prepared, decays],
        compiler_params=_cparams("parallel", "arbitrary"), name="gdn_chain",
    )(qkv, qkv, qkv, p, qkv, qkv, qkv, p, gate_params)


def _outproj_kernel(n_experts, om_ref, rf_ref, rb_ref, rg_ref, gf_ref, gb_ref, z_ref, x_ref, gm_ref, gr_ref,
                    gg_ref, w_ref, gt_ref, g2_ref, sc_ref, sh_ref, wr_ref, br_ref, xo_ref, hp_ref, ti_ref, tw_ref):
    ret, gdn = [], []
    for h in range(RET_HEADS):
        cols = slice(h * LANE, (h + 1) * LANE)
        o = rf_ref[:, cols] + rb_ref[:, cols]
        oc = o - jnp.mean(o, -1, keepdims=True)
        var = jnp.mean(oc * oc, -1, keepdims=True)
        ret.append(oc * lax.rsqrt(var + GROUP_NORM_EPS) * gr_ref[:, cols] * _silu(rg_ref[:, cols]))
    for h in range(GDN_HEADS):
        cols = slice(h * LANE, (h + 1) * LANE)
        o = gf_ref[:, cols] + gb_ref[:, cols]
        gdn.append(_rms(o, gg_ref[...]) * _silu(z_ref[:, cols]))
    mixed = jnp.concatenate([_rms(om_ref[...], gm_ref[...])] + ret + gdn, -1)
    x = x_ref[...] + gt_ref[0] * _mm(mixed, w_ref[...])
    xo_ref[...] = x
    h = (_rms(x, g2_ref[...]) * (1.0 + sc_ref[0]) + sh_ref[0]).astype(MXU_DTYPE)
    hp_ref[...] = h.astype(F32)
    logits = _mm(h, wr_ref[...]) + br_ref[...]
    lane = lax.broadcasted_iota(jnp.int32, logits.shape, 1)
    lane_f = lane.astype(F32)
    neg = jnp.float32(-jnp.inf)
    logits = jnp.where(lane < n_experts, logits, neg)
    vals, idxs = [], []
    for _ in range(TOP_K):
        m = jnp.max(logits, -1, keepdims=True)
        idx = jnp.min(jnp.where(logits == m, lane_f, float(LANE)), -1, keepdims=True)
        logits = jnp.where(lane_f == idx, neg, logits)
        vals.append(m)
        idxs.append(idx.astype(jnp.int32))
    es = [jnp.exp(v - vals[0]) for v in vals]
    den = es[0]
    for e in es[1:]:
        den = den + e
    ti = jnp.zeros(logits.shape, jnp.int32)
    tw = jnp.zeros(logits.shape, F32)
    for k in range(TOP_K):
        ti = jnp.where(lane == k, idxs[k], ti)
        tw = jnp.where(lane == k, es[k] / den, tw)
    ti_ref[...] = ti
    tw_ref[...] = tw


def _outproj(dims, om, rf, rbk, gf, gb, p, x, mod_l, g_mla, g_ret, g_gdn, w_out, g2, w_router_p, b_router_p,
             n_experts):
    t, d, rb = dims["T"], dims["D"], dims["RB"]
    hw = GDN_HEADS * LANE
    row = lambda w: pl.BlockSpec((rb, w), lambda i: (i, 0))
    full = lambda a: pl.BlockSpec(a.shape, lambda i: (0, 0))
    pcol = lambda blk: pl.BlockSpec((rb, hw), lambda i: (i, blk * LANE // hw))
    return pl.pallas_call(
        functools.partial(_outproj_kernel, n_experts),
        grid=(t // rb,),
        in_specs=[row(om.shape[1]), row(hw), row(hw), pcol(PB_RET_G), row(hw), row(hw), pcol(PB_GDN_Z), row(d),
                  full(g_mla), full(g_ret), full(g_gdn), full(w_out),
                  _mod_spec(dims, 2), full(g2), _mod_spec(dims, 4), _mod_spec(dims, 3),
                  full(w_router_p), full(b_router_p)],
        out_specs=[row(d), row(d), row(LANE), row(LANE)],
        out_shape=[jax.ShapeDtypeStruct((t, d), F32), jax.ShapeDtypeStruct((t, d), F32),
                   jax.ShapeDtypeStruct((t, LANE), jnp.int32), jax.ShapeDtypeStruct((t, LANE), F32)],
        compiler_params=_cparams("parallel"), name="outproj_router",
    )(om, rf, rbk, p, gf, gb, p, x, g_mla, g_ret, g_gdn, w_out, mod_l, g2, mod_l, mod_l, w_router_p, b_router_p)


def _expert_kernel(be_ref, nu_ref, x_ref, wgu_ref, bgu_ref, wdn_ref, bdn_ref, y_ref, wgu_s, wdn_s):
    i = pl.program_id(0)
    f = wdn_ref.shape[2]
    prev = be_ref[jnp.maximum(i - 1, 0)]

    @pl.when(jnp.logical_or(i == 0, be_ref[i] != prev))
    def _():
        wgu_s[...] = wgu_ref[0, 0].astype(wgu_s.dtype)
        wdn_s[...] = wdn_ref[0, 0].astype(wdn_s.dtype)

    @pl.when(i < nu_ref[0])
    def _():
        gu = _mm(x_ref[...], wgu_s[...]) + bgu_ref[0]
        gate = jnp.minimum(gu[:, :f], SWIGLU_LIMIT)
        up = jnp.clip(gu[:, f:], -SWIGLU_LIMIT, SWIGLU_LIMIT)
        act = (up + 1.0) * (gate * jax.nn.sigmoid(gate * SWIGLU_ALPHA))
        y_ref[...] = _mm(act, wdn_s[...]) + bdn_ref[0]

    @pl.when(i >= nu_ref[0])
    def _():
        y_ref[...] = jnp.zeros(y_ref.shape, F32)


def _experts(li, xs, block_e, n_used, w_gu, b_gu, w_dn, b_dn):
    rows = xs.shape[0]
    depth, e, d, f2 = w_gu.shape
    f = f2 // 2
    nblk = rows // MOE_BLOCK
    grid_spec = pltpu.PrefetchScalarGridSpec(
        num_scalar_prefetch=2,
        grid=(nblk,),
        in_specs=[pl.BlockSpec((MOE_BLOCK, d), lambda i, be, nu: (i, 0)),
                  pl.BlockSpec((1, 1, d, f2), lambda i, be, nu: (li, be[i], 0, 0)),
                  pl.BlockSpec((1, 1, f2), lambda i, be, nu: (li * e + be[i], 0, 0)),
                  pl.BlockSpec((1, 1, f, d), lambda i, be, nu: (li, be[i], 0, 0)),
                  pl.BlockSpec((1, 1, d), lambda i, be, nu: (li * e + be[i], 0, 0))],
        out_specs=pl.BlockSpec((MOE_BLOCK, d), lambda i, be, nu: (i, 0)),
        scratch_shapes=[pltpu.VMEM((d, f2), MXU_DTYPE), pltpu.VMEM((f, d), MXU_DTYPE)],
    )
    return pl.pallas_call(
        _expert_kernel,
        grid_spec=grid_spec,
        out_shape=jax.ShapeDtypeStruct((rows, d), F32),
        compiler_params=_cparams("arbitrary"), name="experts",
    )(block_e, n_used, xs, w_gu, b_gu.reshape(depth * e, 1, f2), w_dn, b_dn.reshape(depth * e, 1, d))


def _row_token_kernel(d_ref, dt_ref, o_ref, acc_ref):
    i = pl.program_id(0)
    rb = d_ref.shape[0]
    nr = acc_ref.shape[0]

    @pl.when(i == 0)
    def _():
        acc_ref[...] = jnp.zeros(acc_ref.shape, F32)

    d = d_ref[...]
    dt = dt_ref[...]
    tok = i * rb + lax.broadcasted_iota(jnp.int32, (rb, 1), 0)
    hi = (tok >> LANE_BITS).astype(F32)
    lo = (tok & (LANE - 1)).astype(F32)
    sub_r = lax.broadcasted_iota(jnp.int32, (nr, rb), 0)
    lane_c = lax.broadcasted_iota(jnp.int32, (rb, 2 * LANE), 1)
    acc = acc_ref[...]
    for k in range(TOP_K):
        row_hot_t = jnp.where(sub_r == (dt[k:k + 1, :] >> LANE_BITS), 1.0, 0.0)
        col = d[:, k:k + 1] & (LANE - 1)
        col_val = jnp.where(lane_c == col, hi, jnp.where(lane_c == col + LANE, lo, 0.0))
        acc = acc + _mm(row_hot_t, col_val)
    acc_ref[...] = acc

    @pl.when(i == pl.num_programs(0) - 1)
    def _():
        o_ref[...] = (acc[:, :LANE] * float(LANE) + acc[:, LANE:]).astype(jnp.int32)


def _row_tokens(dest, rows, rb):
    t = dest.shape[0]
    assert rows % LANE == 0 and t % rb == 0
    nr = -(-(rows // LANE) // LANE) * LANE
    dest_p = jnp.pad(dest, ((0, 0), (0, LANE - TOP_K)))
    dest_t = jnp.pad(dest.T, ((0, 8 - TOP_K), (0, 0)))
    out = pl.pallas_call(
        _row_token_kernel,
        grid=(t // rb,),
        in_specs=[pl.BlockSpec((rb, LANE), lambda i: (i, 0)), pl.BlockSpec((8, rb), lambda i: (0, i))],
        out_specs=pl.BlockSpec((nr, LANE), lambda i: (0, 0)),
        out_shape=jax.ShapeDtypeStruct((nr, LANE), jnp.int32),
        scratch_shapes=[pltpu.VMEM((nr, 2 * LANE), F32)],
        compiler_params=_cparams("arbitrary"), name="row_tokens",
    )(dest_p, dest_t)
    return out.reshape(-1)[:rows]


def _moe(li, hp, top_i, w_gu, b_gu, w_dn, b_dn):
    t = hp.shape[0]
    e = w_gu.shape[1]
    idx = top_i[:, :TOP_K]
    onehot = idx[:, :, None] == jnp.arange(e, dtype=jnp.int32)[None, None, :]
    member = jnp.sum(onehot, 1).astype(jnp.int32)
    counts = jnp.sum(member, 0)
    rank = jnp.cumsum(member, 0) - member
    padded = (counts + MOE_BLOCK - 1) // MOE_BLOCK * MOE_BLOCK
    pend = jnp.cumsum(padded)
    pstart = pend - padded
    dest = jnp.sum(jnp.where(onehot, (rank + pstart[None, :])[:, None, :], 0), -1)
    nblk = -(-(t * TOP_K) // MOE_BLOCK) + e
    rows = nblk * MOE_BLOCK
    xs = hp[_row_tokens(dest, rows, math.gcd(ROW_BLOCK, t))]
    blk_row = jnp.arange(nblk, dtype=jnp.int32) * MOE_BLOCK
    block_e = jnp.minimum(jnp.sum(pend[None, :] <= blk_row[:, None], 1), e - 1).astype(jnp.int32)
    n_used = (pend[-1] // MOE_BLOCK).astype(jnp.int32).reshape(1)
    ys = _experts(li, xs, block_e, n_used, w_gu, b_gu, w_dn, b_dn)
    return ys[dest.T.reshape(-1)].reshape(TOP_K, t, ys.shape[1])


def _final_kernel(x_ref, tw_ref, *refs):
    y_refs = refs[:TOP_K]
    gt_ref, g_ref, o_ref = refs[TOP_K:]
    o_ref[...] = _rms(x_ref[...] + gt_ref[0] * _combine(tw_ref, y_refs), g_ref[...])


def _final(dims, x, moe, mod_l, g_final):
    b, s, d, rb, l, ct = dims["B"], dims["S"], dims["D"], dims["RB"], dims["L"], dims["CT"]
    ys, tw = moe
    ns = s // rb
    src = lambda i: (i // ns) * (l // rb) + ct // rb + i % ns
    return pl.pallas_call(
        _final_kernel,
        grid=(b * ns,),
        in_specs=[pl.BlockSpec((rb, d), lambda i: (src(i), 0)), pl.BlockSpec((rb, LANE), lambda i: (src(i), 0))]
        + _expert_row_specs(rb, d, src)
        + [pl.BlockSpec((1, 1, d), lambda i: (i // ns, 0, 5)), pl.BlockSpec((1, d), lambda i: (0, 0))],
        out_specs=pl.BlockSpec((rb, d), lambda i: (i, 0)),
        out_shape=jax.ShapeDtypeStruct((b * s, d), F32),
        compiler_params=_cparams("parallel"), name="final_norm",
    )(x, tw, *([ys] * TOP_K), mod_l, g_final)


def _rope_tables(s, rb, quarter, width):
    rows = (jnp.arange(s, dtype=jnp.int32) // GRID_W).astype(F32)
    cols = (jnp.arange(s, dtype=jnp.int32) % GRID_W).astype(F32)
    inv = ROPE_BASE ** (-jnp.arange(quarter, dtype=F32) / quarter)
    ar = rows[:, None] * inv[None, :]
    ac = cols[:, None] * inv[None, :]
    pad = width - 4 * quarter
    cos = jnp.concatenate([jnp.cos(ar), jnp.cos(ar), jnp.cos(ac), jnp.cos(ac), jnp.ones((s, pad), F32)], -1)
    sin = jnp.concatenate([-jnp.sin(ar), jnp.sin(ar), -jnp.sin(ac), jnp.sin(ac), jnp.zeros((s, pad), F32)], -1)
    cos = jnp.concatenate([cos, jnp.ones((rb, width), F32)], 0)
    sin = jnp.concatenate([sin, jnp.zeros((rb, width), F32)], 0)
    return cos, sin


def _layout_w_in(w):
    d = w.shape[0]
    n_mla = MLA_Q_LORA + MLA_KV_LORA + MLA_ROPE
    n_mid = 4 * RET_HEADS * RET_DK + 4 * GDN_HEADS * GDN_DK
    z = lambda n: jnp.zeros((d, n), w.dtype)
    out = jnp.concatenate([w[:, :n_mla], z(PB_RET_Q * LANE - n_mla), w[:, n_mla:n_mla + n_mid],
                           w[:, n_mla + n_mid:], z(LANE - 4 * GDN_HEADS)], -1)
    assert out.shape[1] == P_BLOCKS * LANE
    return out.astype(MXU_DTYPE)


def _layout_w_uq(w):
    r = w.shape[0]
    w = w.reshape(r, MLA_HEADS, MLA_NOPE + MLA_ROPE)
    w = jnp.concatenate([w, jnp.zeros((r, MLA_HEADS, MLA_QK - MLA_NOPE - MLA_ROPE), w.dtype)], -1)
    return w.reshape(r, MLA_HEADS * MLA_QK).astype(MXU_DTYPE)


def _layout_w_ukv(w):
    r = w.shape[0]
    w = w.reshape(r, MLA_HEADS, MLA_NOPE + MLA_V)
    wk = w[:, :, :MLA_NOPE].reshape(r, MLA_HEADS * MLA_NOPE)
    wv = w[:, :, MLA_NOPE:].reshape(r, MLA_HEADS * MLA_V)
    return wk.astype(MXU_DTYPE), wv.astype(MXU_DTYPE)


def kernel(x, c, ctx, c_ctx, w_mod, b_mod, g_norm1, g_norm2, w_in, g_cq, g_ckv, w_uq, w_ukv, g_mla, ret_decay, g_ret, gdn_conv, gdn_a_log, gdn_dt_bias, g_gdn, w_out, w_router, b_router, w_gate_up, b_gate_up, w_down, b_down, g_final):
    b, s, d = x.shape
    ct = ctx.shape[1]
    depth = w_mod.shape[0]
    n_experts = w_router.shape[-1]
    rb = math.gcd(ROW_BLOCK, ct)
    l = ct + s
    dims = dict(B=b, S=s, D=d, CT=ct, L=l, T=b * l, RB=rb)
    assert b + 1 <= 8 and ct % GDN_BLOCK == 0 and s % rb == 0 and n_experts <= LANE
    assert ct % RET_CHUNK == 0 and s % RET_CHUNK == 0

    stream = jnp.concatenate([ctx, x], 1).reshape(b * l, d)
    cs = jnp.concatenate([c, c_ctx[None, :], jnp.zeros((7 - b, d), F32)], 0)
    mod = _modulation(cs, w_mod, b_mod).reshape(depth, 8, 1, 6 * d)

    cos_m, sin_m = _rope_tables(s, rb, MLA_ROPE // 4, LANE)
    cos_r, sin_r = _rope_tables(s, RET_CHUNK, RET_DK // 4, LANE)
    moe = None
    mod_prev = None
    for li in range(depth):
        w_in_p = _layout_w_in(w_in[li])
        stream, p = _inproj(dims, stream, moe, mod[li], mod_prev, g_norm1[li][None, :], w_in_p)
        wk, wv = _layout_w_ukv(w_ukv[li])
        q_m, k_m, v_m = _mla_prep(dims, p, g_cq[li][None, :], g_ckv[li][None, :], _layout_w_uq(w_uq[li]),
                                  wk, wv, cos_m, sin_m)
        o_mla = _attention(dims, q_m, k_m, v_m)
        o_rf, o_rb = _retention(dims, p, cos_r, sin_r, _retention_tables(ret_decay[li]))
        conv_w = jnp.zeros((8, 3 * GDN_HEADS * LANE), F32).at[:GDN_CONV].set(gdn_conv[li])
        conv_w = conv_w.reshape(8, 3 * GDN_HEADS, LANE).transpose(1, 0, 2)
        gate_params = jnp.zeros((8, LANE), F32)
        gate_params = gate_params.at[0, :2 * GDN_HEADS].set(gdn_a_log[li].reshape(-1))
        gate_params = gate_params.at[1, :2 * GDN_HEADS].set(gdn_dt_bias[li].reshape(-1))
        o_gf, o_gb = _gdn_chain(dims, p, _gdn_conv(dims, p, conv_w), gate_params)
        w_router_p = jnp.zeros((d, LANE), F32).at[:, :n_experts].set(w_router[li]).astype(MXU_DTYPE)
        b_router_p = jnp.zeros((1, LANE), F32).at[0, :n_experts].set(b_router[li])
        stream, hp, top_i, top_w = _outproj(dims, o_mla, o_rf, o_rb, o_gf, o_gb, p, stream, mod[li],
                                            g_mla[li][None, :], g_ret[li][None, :], g_gdn[li][None, :],
                                            w_out[li].astype(MXU_DTYPE),
                                            g_norm2[li][None, :], w_router_p, b_router_p, n_experts)
        moe = (_moe(li, hp, top_i, w_gate_up, b_gate_up, w_down, b_down), top_w)
        mod_prev = mod[li]
    return _final(dims, stream, moe, mod_prev, g_final[None, :]).reshape(b, s, d)
```
